```python
import math
import jax, jax.numpy as jnp
from jax import lax
import numpy as np

D_MODEL = 1024
BATCH = 16
SEQ = 256
DEPTH = 4
DEC_BATCH = 4
DEC_SEQ = 4096
PAST_LEN = 256

GRID_W = 64
N_DIR = 2
GDN_HEADS = 4
GDN_DK = 128
GDN_DV = 128
GDN_W = GDN_HEADS * GDN_DV
GDN_CONV = 5
GDN_CHUNK = 64
HY_W = D_MODEL - GDN_W
HY_CONV = 3
HY_EMB = 33
HY_FH = 64
HY_DECAY_TARGET = 1e-2
HY_FAST_PCT = 0.3
HY_SLOW_PCT = 1.5
D_FF = 4 * D_MODEL
EPS = 1e-6
OFF_G = 3 * GDN_W
OFF_A = 4 * GDN_W
OFF_B = OFF_A + N_DIR * GDN_HEADS
OFF_HY = OFF_B + N_DIR * GDN_HEADS
IN_COLS = OFF_HY + 3 * HY_W

kernel_name = "hybrid_gdn_hyena_diffusion_step"

F32 = jnp.float32


def rms_norm(x, g):
    xf = x.astype(F32)
    y = xf * lax.rsqrt(jnp.mean(xf * xf, axis=-1, keepdims=True) + EPS)
    return (y * g.astype(F32)).astype(x.dtype)


def l2norm(x):
    return x * lax.rsqrt(jnp.sum(x * x, axis=-1, keepdims=True) + EPS)


def short_conv(x, w, n_rows):
    b, l, ch = x.shape
    width = w.shape[0]
    pad = width // 2
    seg = l // n_rows
    w = w.astype(F32)
    xr = jnp.pad(x.reshape(b * n_rows, seg, ch), ((0, 0), (pad, pad), (0, 0)))
    y = sum(xr[:, j:j + seg] * w[j] for j in range(width))
    return y.reshape(b, l, ch)


def gdn_scan(q, k, v, g, beta, s0):
    b, l, h, dk = q.shape
    dv = v.shape[-1]
    c = GDN_CHUNK
    n = l // c

    def blk(t):
        return jnp.moveaxis(t.reshape((b, n, c) + t.shape[2:]), 3, 1)

    q_, k_, v_, g_, be_ = blk(q), blk(k), blk(v), blk(g), blk(beta)
    gc = jnp.cumsum(g_, axis=-1)
    incl = jnp.tril(jnp.ones((c, c), dtype=bool))
    strict = jnp.tril(jnp.ones((c, c), dtype=F32), -1)
    decay = jnp.exp(jnp.where(incl, gc[..., :, None] - gc[..., None, :], -jnp.inf))
    kb = k_ * be_[..., None]
    vb = v_ * be_[..., None]
    a_mat = jnp.einsum('bhnid,bhnjd->bhnij', kb, k_) * decay * strict
    m_mat = a_mat + jnp.eye(c, dtype=F32)
    u = lax.linalg.triangular_solve(m_mat, vb, left_side=True, lower=True, unit_diagonal=True)
    w = lax.linalg.triangular_solve(m_mat, kb * jnp.exp(gc)[..., None], left_side=True, lower=True,
                                    unit_diagonal=True)
    qk = jnp.einsum('bhnid,bhnjd->bhnij', q_, k_) * decay
    g_last = gc[..., -1]
    q_dec = q_ * jnp.exp(gc)[..., None]
    k_dec = k_ * jnp.exp(g_last[..., None] - gc)[..., None]
    xs = tuple(jnp.moveaxis(t, 2, 0) for t in (q_dec, k_dec, u, w, qk, g_last))

    def step(s, inp):
        qd, kd, uu, ww, qkk, gl = inp
        v_new = uu - jnp.einsum('bhcd,bhde->bhce', ww, s)
        o = jnp.einsum('bhcd,bhde->bhce', qd, s) + jnp.einsum('bhij,bhje->bhie', qkk, v_new)
        s = s * jnp.exp(gl)[..., None, None] + jnp.einsum('bhcd,bhce->bhde', kd, v_new)
        return s, o

    s_fin, o = lax.scan(step, s0, xs)
    o = jnp.moveaxis(o, 0, 2).transpose(0, 2, 3, 1, 4).reshape(b, l, h, dv)
    return o, s_fin


def hyena_filter(l, w1, b1, freq, w2, b2, w3, b3):
    t = jnp.linspace(0.0, 1.0, l, dtype=F32)[:, None]
    bands = (HY_EMB - 1) // 2
    f = jnp.linspace(1e-4, bands - 1, bands, dtype=F32)[None, :]
    wpos = (2.0 * math.pi) * jnp.arange(l, dtype=F32)[:, None] / l
    z = jnp.concatenate([t, jnp.cos(f * wpos), -jnp.sin(f * wpos)], axis=-1)
    freq = freq.astype(F32)
    h = jnp.sin(freq * (z @ w1.astype(F32) + b1.astype(F32)))
    h = jnp.sin(freq * (h @ w2.astype(F32) + b2.astype(F32)))
    h = (h @ w3.astype(F32) + b3.astype(F32)).reshape(l, N_DIR, HY_W)
    max_decay = math.log(HY_DECAY_TARGET) / HY_FAST_PCT
    min_decay = math.log(HY_DECAY_TARGET) / HY_SLOW_PCT
    deltas = jnp.abs(jnp.linspace(min_decay, max_decay, HY_W, dtype=F32))
    h = h * jnp.exp(-t[:, :, None] * deltas)
    filt = jnp.concatenate([h[:, 0], jnp.zeros((1, HY_W), F32), h[:0:-1, 1]], axis=0)
    return jnp.fft.rfft(filt, axis=0)


def trunk_layer(x, mod, s0, n_rows, p):
    b, l, _ = x.shape
    shift1, scale1, gate1, shift2, scale2, gate2 = jnp.split(mod, 6, axis=-1)
    hn = rms_norm(x, p['norm1_g']) * (1 + scale1[:, None]) + shift1[:, None]
    proj = (hn @ p['w_in']).astype(F32)

    qkv = jax.nn.silu(short_conv(proj[..., :OFF_G], p['gdn_conv_w'], n_rows))
    q, k, v = jnp.split(qkv, 3, axis=-1)
    q = l2norm(q.reshape(b, l, GDN_HEADS, GDN_DK)) * (GDN_DK ** -0.5)
    k = l2norm(k.reshape(b, l, GDN_HEADS, GDN_DK))
    v = v.reshape(b, l, GDN_HEADS, GDN_DV)
    a_logit = proj[..., OFF_A:OFF_B].reshape(b, l, N_DIR, GDN_HEADS)
    b_logit = proj[..., OFF_B:OFF_HY].reshape(b, l, N_DIR, GDN_HEADS)
    g = -jnp.exp(p['gdn_a_log'].astype(F32)) * jax.nn.softplus(a_logit + p['gdn_dt_bias'].astype(F32))
    beta = jax.nn.sigmoid(b_logit)
    s0 = s0.astype(F32)
    o_f, s_f = gdn_scan(q, k, v, g[:, :, 0], beta[:, :, 0], s0[:, 0])
    flip = lambda t: t[:, ::-1]
    o_b, s_b = gdn_scan(flip(q), flip(k), flip(v), flip(g[:, :, 1]), flip(beta[:, :, 1]), s0[:, 1])
    o = o_f + flip(o_b)
    gate_o = proj[..., OFF_G:OFF_A].reshape(b, l, GDN_HEADS, GDN_DV)
    o = (rms_norm(o, p['gdn_norm_g']) * jax.nn.silu(gate_o)).reshape(b, l, GDN_W)

    hy = short_conv(proj[..., OFF_HY:], p['hy_conv_w'], n_rows)
    x0, x1, hv = jnp.split(hy, 3, axis=-1)
    zz = x1 * hv
    spec = hyena_filter(l, p['hy_w1'], p['hy_b1'], p['hy_freq'], p['hy_w2'], p['hy_b2'],
                        p['hy_w3'], p['hy_b3'])
    zc = jnp.fft.irfft(jnp.fft.rfft(zz, n=2 * l, axis=1) * spec[None], n=2 * l, axis=1)[:, :l]
    y_h = x0 * (zc + zz * p['hy_skip'].astype(F32))

    mix = jnp.concatenate([o, y_h], axis=-1).astype(x.dtype) @ p['w_out']
    x = x + gate1[:, None] * mix

    h2 = rms_norm(x, p['norm2_g']) * (1 + scale2[:, None]) + shift2[:, None]
    x = x + gate2[:, None] * (jnp.square(jax.nn.relu(h2 @ p['w_mlp1'])) @ p['w_mlp2'])
    return x, jnp.stack([s_f, s_b], axis=1)


def setup_inputs(seed: int = 0) -> dict:
    key = jax.random.key(seed)
    ks = jax.random.split(key, 32)
    nrm = lambda k, shape, s: jax.random.normal(k, shape, F32) * s
    dt = jnp.exp(jax.random.uniform(ks[12], (DEPTH, N_DIR, GDN_HEADS), F32, math.log(1e-3), math.log(1e-1)))
    return {
        "x_prompt": nrm(ks[0], (BATCH, SEQ, D_MODEL), 1.0),
        "x_sample": nrm(ks[1], (DEC_BATCH, DEC_SEQ, D_MODEL), 1.0),
        "state_gdn": nrm(ks[2], (DEC_BATCH, DEPTH, N_DIR, GDN_HEADS, GDN_DK, GDN_DV), GDN_DK ** -0.5),
        "c": nrm(ks[3], (DEC_BATCH, D_MODEL), 1.0),
        "c_ctx": nrm(ks[4], (D_MODEL,), 1.0),
        "w_ada": nrm(ks[5], (DEPTH, D_MODEL, 6 * D_MODEL), 0.5 * D_MODEL ** -0.5),
        "b_ada": nrm(ks[6], (DEPTH, 6 * D_MODEL), 0.02),
        "norm1_g": 1.0 + nrm(ks[7], (DEPTH, D_MODEL), 0.02),
        "norm2_g": 1.0 + nrm(ks[8], (DEPTH, D_MODEL), 0.02),
        "w_in": nrm(ks[9], (DEPTH, D_MODEL, IN_COLS), D_MODEL ** -0.5),
        "gdn_conv_w": nrm(ks[10], (DEPTH, GDN_CONV, 3 * GDN_W), GDN_CONV ** -0.5),
        "gdn_a_log": jnp.log(jax.random.uniform(ks[11], (DEPTH, N_DIR, GDN_HEADS), F32, 1.0, 16.0)),
        "gdn_dt_bias": dt + jnp.log(-jnp.expm1(-dt)),
        "gdn_norm_g": 1.0 + nrm(ks[13], (DEPTH, GDN_DV), 0.02),
        "hy_conv_w": nrm(ks[14], (DEPTH, HY_CONV, 3 * HY_W), HY_CONV ** -0.5),
        "hy_w1": nrm(ks[15], (DEPTH, HY_EMB, HY_FH), HY_EMB ** -0.5),
        "hy_b1": nrm(ks[16], (DEPTH, HY_FH), 0.02),
        "hy_freq": 1.0 + nrm(ks[17], (DEPTH, HY_FH), 0.02),
        "hy_w2": nrm(ks[18], (DEPTH, HY_FH, HY_FH), HY_FH ** -0.5),
        "hy_b2": nrm(ks[19], (DEPTH, HY_FH), 0.02),
        "hy_w3": nrm(ks[20], (DEPTH, HY_FH, N_DIR * HY_W), 0.05 * HY_FH ** -0.5),
        "hy_b3": nrm(ks[21], (DEPTH, N_DIR * HY_W), 0.01),
        "hy_skip": nrm(ks[22], (DEPTH, HY_W), 0.5),
        "w_out": nrm(ks[23], (DEPTH, D_MODEL, D_MODEL), D_MODEL ** -0.5),
        "w_mlp1": nrm(ks[24], (DEPTH, D_MODEL, D_FF), D_MODEL ** -0.5),
        "w_mlp2": nrm(ks[25], (DEPTH, D_FF, D_MODEL), D_FF ** -0.5),
        "final_g": 1.0 + nrm(ks[26], (D_MODEL,), 0.02),
    }


def reference(x_prompt, x_sample, state_gdn, c, c_ctx, w_ada, b_ada, norm1_g, norm2_g, w_in,
              gdn_conv_w, gdn_a_log, gdn_dt_bias, gdn_norm_g, hy_conv_w, hy_w1, hy_b1, hy_freq,
              hy_w2, hy_b2, hy_w3, hy_b3, hy_skip, w_out, w_mlp1, w_mlp2, final_g):
    rows = x_sample.shape[1] // GRID_W

    def layer_params(l):
        return {"norm1_g": norm1_g[l], "norm2_g": norm2_g[l], "w_in": w_in[l],
                "gdn_conv_w": gdn_conv_w[l], "gdn_a_log": gdn_a_log[l], "gdn_dt_bias": gdn_dt_bias[l],
                "gdn_norm_g": gdn_norm_g[l], "hy_conv_w": hy_conv_w[l], "hy_w1": hy_w1[l],
                "hy_b1": hy_b1[l], "hy_freq": hy_freq[l], "hy_w2": hy_w2[l], "hy_b2": hy_b2[l],
                "hy_w3": hy_w3[l], "hy_b3": hy_b3[l], "hy_skip": hy_skip[l], "w_out": w_out[l],
                "w_mlp1": w_mlp1[l], "w_mlp2": w_mlp2[l]}

    x = x_prompt
    s_zero = jnp.zeros((x_prompt.shape[0], N_DIR, GDN_HEADS, GDN_DK, GDN_DV), F32)
    ctx_states = []
    for l in range(DEPTH):
        mod = (jax.nn.silu(c_ctx) @ w_ada[l] + b_ada[l])[None]
        x, s_l = trunk_layer(x, mod, s_zero, 1, layer_params(l))
        ctx_states.append(s_l)
    y_prompt = rms_norm(x, final_g)
    new_state_gdn = jnp.stack(ctx_states, axis=1).astype(x_prompt.dtype)

    x = x_sample
    for l in range(DEPTH):
        mod = jax.nn.silu(c) @ w_ada[l] + b_ada[l]
        x, _ = trunk_layer(x, mod, state_gdn[:, l], rows, layer_params(l))
    y_sample = rms_norm(x, final_g)

    return (y_prompt, y_sample, new_state_gdn)
```

```python
import functools
import math

import numpy as np
import jax
import jax.numpy as jnp
from jax import lax
from jax.experimental import pallas as pl
from jax.experimental.pallas import tpu as pltpu

F32 = jnp.float32
BF16 = jnp.bfloat16

D_MODEL = 1024
DEPTH = 4
GRID_W = 64
N_DIR = 2
HEADS = 4
DK = 128
GDN_W = HEADS * DK
GDN_CONV = 5
HY_W = D_MODEL - GDN_W
HY_CONV = 3
HY_EMB = 33
HY_FH = 64
D_FF = 4 * D_MODEL
EPS = 1e-6
OFF_G = 3 * GDN_W
OFF_A = 4 * GDN_W
OFF_B = OFF_A + N_DIR * HEADS
OFF_HY = OFF_B + N_DIR * HEADS
IN_COLS = OFF_HY + 3 * HY_W

LANES = 128
CHUNK = 64
PACK = HEADS * CHUNK
GDN_GROUP = 4
TOKEN_TILE = 256
GDN_BLOCK = 1024
HY_BLOCK = 512
HY_CT = 128
MOD_ROWS = 8
VMEM_LIMIT = 56 * 1024 * 1024


def _dot(a, b):
    return jnp.dot(a.astype(BF16), b.astype(BF16), preferred_element_type=F32)


def _dot_nt(a, b):
    return lax.dot_general(a.astype(BF16), b.astype(BF16), (((1,), (1,)), ((), ())),
                           preferred_element_type=F32)


def _dot_tn(a, b):
    return lax.dot_general(a.astype(BF16), b.astype(BF16), (((0,), (0,)), ((), ())),
                           preferred_element_type=F32)


def _split2(x):
    hi = x.astype(BF16)
    lo = (x - hi.astype(F32)).astype(BF16)
    return hi, lo


def _split3(x):
    hi = x.astype(BF16)
    r = x - hi.astype(F32)
    mid = r.astype(BF16)
    lo = (r - mid.astype(F32)).astype(BF16)
    return hi, mid, lo


def _dot_hp(a, b):
    ah, al = _split2(a)
    bh, bl = _split2(b)
    return (jnp.dot(ah, bh, preferred_element_type=F32)
            + (jnp.dot(ah, bl, preferred_element_type=F32)
               + jnp.dot(al, bh, preferred_element_type=F32)))


def _dot_mask(m, x):
    hi, mid, lo = _split3(x)
    return (jnp.dot(m, hi, preferred_element_type=F32)
            + (jnp.dot(m, mid, preferred_element_type=F32)
               + jnp.dot(m, lo, preferred_element_type=F32)))


def _silu(y):
    return y / (1.0 + jnp.exp(-y))


def _sigmoid(y):
    return 1.0 / (1.0 + jnp.exp(-y))


def _full(shape):
    nd = len(shape)
    return pl.BlockSpec(shape, lambda *_: (0,) * nd)


def _resident(shape):
    nd = len(shape)
    return pl.BlockSpec(shape, lambda *_: (0,) * nd, pipeline_mode=pl.Buffered(1))


def _mod_kernel(c_ref, w_ref, b_ref, o_ref):
    c = c_ref[...]
    o_ref[0] = _dot_hp(_silu(c), w_ref[0]) + b_ref[0]


def _modulation(cvec, w_ada, b_ada):
    tn = 1536
    n6 = 6 * D_MODEL
    return pl.pallas_call(
        _mod_kernel,
        grid=(DEPTH, n6 // tn),
        in_specs=[
            pl.BlockSpec((MOD_ROWS, D_MODEL), lambda l, j: (0, 0)),
            pl.BlockSpec((1, D_MODEL, tn), lambda l, j: (l, 0, j)),
            pl.BlockSpec((1, 1, tn), lambda l, j: (l, 0, j)),
        ],
        out_specs=pl.BlockSpec((1, MOD_ROWS, tn), lambda l, j: (l, 0, j)),
        out_shape=jax.ShapeDtypeStruct((DEPTH, MOD_ROWS, n6), F32),
        compiler_params=pltpu.CompilerParams(
            dimension_semantics=("arbitrary", "arbitrary"), vmem_limit_bytes=VMEM_LIMIT),
        name="modulation",
    )(cvec, w_ada, b_ada.reshape(DEPTH, 1, n6))


def _seg_conv(p, w_ref, col0, width, rowmod, seg):
    rows = p.shape[0]
    pad = width // 2
    acc = None
    for j in range(width):
        off = j - pad
        if off == 0:
            term = p
        else:
            src = rowmod + off
            term = jnp.where((src >= 0) & (src < seg), pltpu.roll(p, (-off) % rows, axis=0), 0.0)
        term = term * w_ref[j:j + 1, col0:col0 + LANES]
        acc = term if acc is None else acc + term
    return acc


def _proj_kernel(x_ref, mod_ref, n1g_ref, wq_ref, wab_ref, why_ref, cw_ref, hcw_ref, alog_ref, dtb_ref,
                 qkv_ref, gate_ref, gb_ref, x0_ref, zz_ref, *, seg):
    tm = x_ref.shape[0]
    x = x_ref[...]
    mod = mod_ref[0]
    shift = mod[:, 0:D_MODEL]
    scale = mod[:, D_MODEL:2 * D_MODEL]
    hn = x * lax.rsqrt(jnp.mean(x * x, axis=-1, keepdims=True) + EPS) * n1g_ref[...]
    hn = (hn * (1.0 + scale) + shift).astype(BF16)
    rowmod = lax.broadcasted_iota(jnp.int32, (tm, LANES), 0) & (seg - 1)

    for part in range(3):
        p = jnp.dot(hn, wq_ref[:, part * GDN_W:(part + 1) * GDN_W], preferred_element_type=F32)
        for h in range(HEADS):
            c0 = part * GDN_W + h * DK
            y = _silu(_seg_conv(p[:, h * DK:(h + 1) * DK], cw_ref, c0, GDN_CONV, rowmod, seg))
            if part < 2:
                y = y * lax.rsqrt(jnp.sum(y * y, axis=-1, keepdims=True) + EPS)
            if part == 0:
                y = y * (DK ** -0.5)
            qkv_ref[:, c0:c0 + DK] = y
    gate_ref[...] = jnp.dot(hn, wq_ref[:, OFF_G:OFF_A], preferred_element_type=F32)

    z = jnp.dot(hn, wab_ref[...], preferred_element_type=F32)
    za = z + dtb_ref[...]
    softplus = jnp.maximum(za, 0.0) + jnp.log1p(jnp.exp(-jnp.abs(za)))
    g = -jnp.exp(alog_ref[...]) * softplus
    lane = lax.broadcasted_iota(jnp.int32, (tm, LANES), 1)
    gb_ref[...] = jnp.where(lane < N_DIR * HEADS, g, _sigmoid(z))

    px0 = jnp.dot(hn, why_ref[:, 0:HY_W], preferred_element_type=F32)
    px1 = jnp.dot(hn, why_ref[:, HY_W:2 * HY_W], preferred_element_type=F32)
    pv = jnp.dot(hn, why_ref[:, 2 * HY_W:3 * HY_W], preferred_element_type=F32)
    for cb in range(HY_W // LANES):
        sl = slice(cb * LANES, (cb + 1) * LANES)
        x0_ref[:, sl] = _seg_conv(px0[:, sl], hcw_ref, cb * LANES, HY_CONV, rowmod, seg)
        x1 = _seg_conv(px1[:, sl], hcw_ref, HY_W + cb * LANES, HY_CONV, rowmod, seg)
        hv = _seg_conv(pv[:, sl], hcw_ref, 2 * HY_W + cb * LANES, HY_CONV, rowmod, seg)
        zz_ref[:, sl] = x1 * hv


def _proj(x, mod3, row0, tiles_per_row, seg, p):
    t = x.shape[0]
    tm = TOKEN_TILE
    tok = lambda w: pl.BlockSpec((tm, w), lambda i: (i, 0))
    return pl.pallas_call(
        functools.partial(_proj_kernel, seg=seg),
        grid=(t // tm,),
        in_specs=[
            tok(D_MODEL),
            pl.BlockSpec((1, 1, 6 * D_MODEL), lambda i: (row0 + i // tiles_per_row, 0, 0)),
            _full((1, D_MODEL)),
            _resident((D_MODEL, OFF_A)),
            _resident((D_MODEL, LANES)),
            _resident((D_MODEL, 3 * HY_W)),
            _full((GDN_CONV, 3 * GDN_W)),
            _full((HY_CONV, 3 * HY_W)),
            _full((1, LANES)),
            _full((1, LANES)),
        ],
        out_specs=[tok(3 * GDN_W), tok(GDN_W), tok(LANES), tok(HY_W), tok(HY_W)],
        out_shape=[
            jax.ShapeDtypeStruct((t, 3 * GDN_W), F32),
            jax.ShapeDtypeStruct((t, GDN_W), F32),
            jax.ShapeDtypeStruct((t, LANES), F32),
            jax.ShapeDtypeStruct((t, HY_W), F32),
            jax.ShapeDtypeStruct((t, HY_W), F32),
        ],
        compiler_params=pltpu.CompilerParams(
            dimension_semantics=("parallel",), vmem_limit_bytes=VMEM_LIMIT),
        name="proj",
    )(x, mod3, p["n1g"], p["wq"], p["wab"], p["why"], p["cw"], p["hcw"], p["alog"], p["dtb"])


def _blockdiag(y, masks):
    return jnp.concatenate([jnp.where(m, y, jnp.zeros_like(y)) for m in masks], axis=0)


def _pdot(xs, ys, masks):
    rhs = [_blockdiag(y.astype(BF16), masks) for y in ys]
    return [jnp.dot(x.astype(BF16), r, preferred_element_type=F32) for x, r in zip(xs, rhs)]


def _unit_tri_inverse(a, eye, m16, m32, m64, hm):
    add = lambda ts, ps: [t + p for t, p in zip(ts, ps)]
    sub = lambda ts, ps: [t - p for t, p in zip(ts, ps)]
    n = [-(x * m16) for x in a]
    t = [eye + x for x in n]
    n2 = _pdot(n, n, hm)
    t = add(t, _pdot(t, n2, hm))
    n4 = _pdot(n2, n2, hm)
    t = add(t, _pdot(t, n4, hm))
    n8 = _pdot(n4, n4, hm)
    t = add(t, _pdot(t, n8, hm))
    t = sub(t, _pdot(t, _pdot([x * m32 for x in a], t, hm), hm))
    t = sub(t, _pdot(t, _pdot([x * m64 for x in a], t, hm), hm))
    return t


def _gdn_kernel(q_ref, k_ref, v_ref, gb_ref, s0_ref, o_ref, sfin_ref,
                s_sc, gp_sc, bp_sc, gw_sc, bw_sc, u_sc, w_sc, qd_sc, kd_sc, qk_sc, eg_sc, *, nch):
    d = pl.program_id(1)
    j = pl.program_id(2)
    c = CHUNK
    fwd = d == 0

    @pl.when(j == 0)
    def _():
        s_sc[...] = s0_ref[0, 0]

    row = lax.broadcasted_iota(jnp.int32, (c, PACK), 0)
    lane = lax.broadcasted_iota(jnp.int32, (c, PACK), 1)
    col = lane & (c - 1)
    r = jnp.where(fwd, row, col)
    cc = jnp.where(fwd, col, row)
    tri = r >= cc
    strict_f = (r > cc).astype(F32)
    eye = (row == col).astype(F32)
    m16 = ((row >> 4) == (col >> 4)).astype(F32)
    same32 = ((row >> 5) == (col >> 5)).astype(F32)
    m32 = same32 - m16
    m64 = 1.0 - same32
    hm = [(lane >> 6) == h for h in range(HEADS)]
    lane_w = lax.broadcasted_iota(jnp.int32, (c, GDN_W), 1)
    hw = [(lane_w >> 7) == h for h in range(HEADS)]
    lane_w2 = lax.broadcasted_iota(jnp.int32, (c, 2 * GDN_W), 1)
    hw2 = [((lane_w2 >> 7) & (HEADS - 1)) == h for h in range(HEADS)]
    r2 = lax.broadcasted_iota(jnp.int32, (c, c), 0)
    c2 = lax.broadcasted_iota(jnp.int32, (c, c), 1)
    tri_b = (jnp.where(fwd, r2, c2) >= jnp.where(fwd, c2, r2)).astype(F32).astype(BF16)
    cum_lhs = jnp.concatenate([tri_b, jnp.ones((c, c), BF16)], axis=0)

    gh, gm, gl = _split3(gb_ref[...])

    def spread(width, shift, first_row):
        sel_r = lax.broadcasted_iota(jnp.int32, (LANES, width), 0)
        sel_h = lax.broadcasted_iota(jnp.int32, (LANES, width), 1) >> shift
        sel = (sel_r == first_row + sel_h).astype(F32).astype(BF16)
        return (jnp.dot(gh, sel, preferred_element_type=F32)
                + (jnp.dot(gm, sel, preferred_element_type=F32)
                   + jnp.dot(gl, sel, preferred_element_type=F32)))

    gp_sc[...] = spread(PACK, 6, d * HEADS)
    bp_sc[...] = spread(PACK, 6, (N_DIR + d) * HEADS)
    gw_sc[...] = spread(GDN_W, 7, d * HEADS)
    bw_sc[...] = spread(GDN_W, 7, (N_DIR + d) * HEADS)

    def precompute(n, carry):
        grp = range(GDN_GROUP)
        rows = [pl.ds(pl.multiple_of((n * GDN_GROUP + i) * c, c), c) for i in grp]
        kc = [k_ref[r_, :] for r_ in rows]
        qc = [q_ref[r_, :] for r_ in rows]
        e = [_dot_mask(tri_b, gp_sc[r_, :] * strict_f) for r_ in rows]
        kb = [x.astype(BF16) for x in kc]
        kq = [lax.dot_general(jnp.concatenate([kb[i], qc[i].astype(BF16)], axis=0), _blockdiag(kb[i], hw),
                              (((1,), (1,)), ((), ())), preferred_element_type=F32) for i in grp]
        decay = [jnp.where(tri, jnp.exp(x), 0.0) for x in e]
        a = [bp_sc[rows[i], :] * kq[i][:c] * decay[i] * strict_f for i in grp]
        t = _unit_tri_inverse(a, eye, m16, m32, m64, hm)
        cums = [_dot_mask(cum_lhs, gw_sc[r_, :]) for r_ in rows]
        gc = [x[:c] for x in cums]
        gt = [x[c:] for x in cums]
        egc = [jnp.exp(x) for x in gc]
        bw = [bw_sc[r_, :] for r_ in rows]
        uw = _pdot(t, [jnp.concatenate([v_ref[rows[i], :] * bw[i], kc[i] * bw[i] * egc[i]], axis=1)
                       for i in grp], hw2)
        for i in grp:
            u_sc[rows[i], :] = uw[i][:, :GDN_W]
            w_sc[rows[i], :] = uw[i][:, GDN_W:].astype(BF16)
            qd_sc[rows[i], :] = (qc[i] * egc[i]).astype(BF16)
            kd_sc[rows[i], :] = (kc[i] * jnp.exp(gt[i] - gc[i])).astype(BF16)
            qk_sc[rows[i], :] = (kq[i][c:] * decay[i]).astype(BF16)
            eg_sc[pl.ds(pl.multiple_of((n * GDN_GROUP + i) * 8, 8), 8), :] = jnp.exp(gt[i][0:8, :])
        return carry

    lax.fori_loop(0, nch // GDN_GROUP, precompute, 0)

    def recur(n, carry):
        ne = jnp.where(fwd, n, nch - 1 - n)
        rows = pl.ds(pl.multiple_of(ne * c, c), c)
        v_new = []
        q_s = []
        for h in range(HEADS):
            hs = slice(h * DK, (h + 1) * DK)
            both = jnp.dot(jnp.concatenate([w_sc[rows, hs], qd_sc[rows, hs]], axis=0),
                           s_sc[h].astype(BF16), preferred_element_type=F32)
            v_new.append(u_sc[rows, hs] - both[:c])
            q_s.append(both[c:])
        vb = jnp.concatenate(v_new, axis=1).astype(BF16)
        o_ref[0, 0, rows, :] = jnp.concatenate(q_s, axis=1) + jnp.dot(
            qk_sc[rows, :], _blockdiag(vb, hw), preferred_element_type=F32)
        eg = eg_sc[pl.ds(pl.multiple_of(ne * 8, 8), 8), :][0:1, :]
        for h in range(HEADS):
            hs = slice(h * DK, (h + 1) * DK)
            s_sc[h] = s_sc[h] * eg[:, hs] + lax.dot_general(
                kd_sc[rows, hs], vb[:, hs], (((0,), (0,)), ((), ())), preferred_element_type=F32)
        return carry

    lax.fori_loop(0, nch, recur, 0)

    @pl.when(j == pl.num_programs(2) - 1)
    def _():
        sfin_ref[0, 0] = s_sc[...]


def _gdn(qkv, gb, s0, nb, seq):
    lb = min(seq, GDN_BLOCK)
    nblk = seq // lb
    nch = lb // CHUNK

    def blk(b, d, j):
        return b * nblk + j + d * (nblk - 1 - 2 * j)

    def qkv_spec(part):
        return pl.BlockSpec((lb, GDN_W), lambda b, d, j: (blk(b, d, j), part))

    state_spec = pl.BlockSpec((1, 1, HEADS, DK, DK), lambda b, d, j: (b, d, 0, 0, 0))
    return pl.pallas_call(
        functools.partial(_gdn_kernel, nch=nch),
        grid=(nb, N_DIR, nblk),
        in_specs=[
            qkv_spec(0), qkv_spec(1), qkv_spec(2),
            pl.BlockSpec((lb, LANES), lambda b, d, j: (blk(b, d, j), 0)),
            state_spec,
        ],
        out_specs=[
            pl.BlockSpec((1, 1, lb, GDN_W), lambda b, d, j: (b, d, j + d * (nblk - 1 - 2 * j), 0)),
            state_spec,
        ],
        out_shape=[
            jax.ShapeDtypeStruct((nb, N_DIR, seq, GDN_W), F32),
            jax.ShapeDtypeStruct((nb, N_DIR, HEADS, DK, DK), F32),
        ],
        scratch_shapes=[
            pltpu.VMEM((HEADS, DK, DK), F32),
            pltpu.VMEM((lb, PACK), F32),
            pltpu.VMEM((lb, PACK), F32),
            pltpu.VMEM((lb, GDN_W), F32),
            pltpu.VMEM((lb, GDN_W), F32),
            pltpu.VMEM((lb, GDN_W), F32),
            pltpu.VMEM((lb, GDN_W), BF16),
            pltpu.VMEM((lb, GDN_W), BF16),
            pltpu.VMEM((lb, GDN_W), BF16),
            pltpu.VMEM((lb, PACK), BF16),
            pltpu.VMEM((nch * 8, GDN_W), F32),
        ],
        compiler_params=pltpu.CompilerParams(
            dimension_semantics=("parallel", "arbitrary", "arbitrary"), vmem_limit_bytes=VMEM_LIMIT),
        name="gdn",
    )(qkv, qkv, qkv, gb, s0)


def _hy_dims(seq):
    nb = min(HY_BLOCK, seq)
    n1 = 2 * seq // nb
    return nb, n1, n1 // 2, n1 // 2 + 1


def _hy_consts(seq):
    nb, n1, nt1, nk = _hy_dims(seq)
    n = 2 * seq
    k2 = np.arange(nb)[:, None]
    t2 = np.arange(nb)[None, :]
    ang = 2.0 * np.pi * ((k2 * t2) % nb) / nb
    fr, fi = np.cos(ang), -np.sin(ang)
    fwd = np.block([[fr, -fi], [fi, fr]])
    inv = np.block([[fr, fi], [-fi, fr]])
    k1 = np.arange(nk)[:, None]
    ang = 2.0 * np.pi * (k1 * np.arange(nb)[None, :]) / n
    twr = np.repeat(np.cos(ang)[:, :, None], LANES, axis=2)
    twi = np.repeat(-np.sin(ang)[:, :, None], LANES, axis=2)
    ang = 2.0 * np.pi * ((k1 * np.arange(n1)[None, :]) % n1) / n1
    return dict(
        fwd=jnp.asarray(fwd, F32).astype(BF16), inv=jnp.asarray(inv, F32).astype(BF16),
        twr=jnp.asarray(twr, F32), twi=jnp.asarray(twi, F32),
        cos=jnp.asarray(np.cos(ang), F32), sin=jnp.asarray(np.sin(ang), F32))


def _hy_forward(block, k1, cos_ref, sin_ref, twr_ref, twi_ref, fwd_ref, nt1, nb):
    cr = None
    ci = None
    for t1 in range(nt1):
        xb = block(t1)
        tr = cos_ref[k1, t1] * xb
        ti = sin_ref[k1, t1] * xb
        cr = tr if cr is None else cr + tr
        ci = ti if ci is None else ci + ti
    ci = -ci
    twr = twr_ref[k1]
    twi = twi_ref[k1]
    c2 = jnp.concatenate([cr * twr - ci * twi, cr * twi + ci * twr], axis=0).astype(BF16)
    y = jnp.dot(fwd_ref[...], c2, preferred_element_type=F32)
    return y[:nb], y[nb:]


def _hyena_kernel(cos_ref, sin_ref, zz_ref, x0_ref, skip_ref, hr_ref, hi_ref, twr_ref, twi_ref,
                  fwd_ref, inv_ref, o_ref, zc_sc, *, nb, n1, nt1, nk):
    zc_sc[...] = jnp.zeros_like(zc_sc)

    def body(k1, carry):
        yr, yi = _hy_forward(lambda t1: zz_ref[t1 * nb:(t1 + 1) * nb, :], k1,
                             cos_ref, sin_ref, twr_ref, twi_ref, fwd_ref, nt1, nb)
        hr = hr_ref[k1]
        hi = hi_ref[k1]
        z2 = jnp.concatenate([yr * hr - yi * hi, yr * hi + yi * hr], axis=0).astype(BF16)
        dd = jnp.dot(inv_ref[...], z2, preferred_element_type=F32)
        dr, di = dd[:nb], dd[nb:]
        twr = twr_ref[k1]
        twi = twi_ref[k1]
        er = dr * twr + di * twi
        ei = di * twr - dr * twi
        wt = jnp.where((k1 == 0) | (k1 == n1 // 2), 1.0, 2.0) / (n1 * nb)
        for t1 in range(nt1):
            rows = slice(t1 * nb, (t1 + 1) * nb)
            zc_sc[rows, :] += (wt * cos_ref[k1, t1]) * er - (wt * sin_ref[k1, t1]) * ei
        return carry

    lax.fori_loop(0, nk, body, 0)
    zz = zz_ref[...]
    o_ref[...] = x0_ref[...] * (zc_sc[...] + zz * skip_ref[...])


def _hyena(zz, x0, skip, hr, hi, consts, nb_batch, seq):
    nb, n1, nt1, nk = _hy_dims(seq)
    ct = HY_CT
    nct = HY_W // ct
    smem = pl.BlockSpec(memory_space=pltpu.SMEM)
    tok = pl.BlockSpec((seq, ct), lambda c, b: (b, c))
    spec = pl.BlockSpec((nk, nb, ct), lambda c, b: (0, 0, c))
    return pl.pallas_call(
        functools.partial(_hyena_kernel, nb=nb, n1=n1, nt1=nt1, nk=nk),
        grid=(nct, nb_batch),
        in_specs=[
            smem, smem, tok, tok,
            pl.BlockSpec((1, ct), lambda c, b: (0, c)),
            spec, spec,
            _resident((nk, nb, LANES)), _resident((nk, nb, LANES)),
            _resident((2 * nb, 2 * nb)), _resident((2 * nb, 2 * nb)),
        ],
        out_specs=tok,
        out_shape=jax.ShapeDtypeStruct((nb_batch * seq, HY_W), F32),
        scratch_shapes=[pltpu.VMEM((seq, ct), F32)],
        compiler_params=pltpu.CompilerParams(
            dimension_semantics=("parallel", "parallel"), vmem_limit_bytes=VMEM_LIMIT),
        name="hyena",
    )(consts["cos"], consts["sin"], zz, x0, skip, hr, hi, consts["twr"], consts["twi"],
      consts["fwd"], consts["inv"])


def _filter_kernel(cos_ref, sin_ref, z_ref, tcol_ref, w1_ref, b1_ref, fq_ref, w2_ref, b2_ref,
                   w3a_ref, w3b_ref, b3a_ref, b3b_ref, dl_ref, twr_ref, twi_ref, fwd_ref,
                   hr_ref, hi_ref, h0_sc, h1_sc, *, nb, n1, nk):
    fq = fq_ref[0]
    a1 = jnp.sin(fq * (_dot_hp(z_ref[...], w1_ref[0]) + b1_ref[0]))
    a2 = jnp.sin(fq * (_dot_hp(a1, w2_ref[0]) + b2_ref[0]))
    win = jnp.exp(-tcol_ref[...] * dl_ref[...])
    h0_sc[...] = (_dot_hp(a2, w3a_ref[0]) + b3a_ref[0]) * win
    h1 = (_dot_hp(a2, w3b_ref[0]) + b3b_ref[0]) * win
    rowid = lax.broadcasted_iota(jnp.int32, h1.shape, 0)
    h1_sc[...] = jnp.where(rowid == 0, 0.0, h1)

    def body(k1, carry):
        ar, ai = _hy_forward(lambda t1: h0_sc[t1 * nb:(t1 + 1) * nb, :], k1,
                             cos_ref, sin_ref, twr_ref, twi_ref, fwd_ref, n1 // 2, nb)
        br, bi = _hy_forward(lambda t1: h1_sc[t1 * nb:(t1 + 1) * nb, :], k1,
                             cos_ref, sin_ref, twr_ref, twi_ref, fwd_ref, n1 // 2, nb)
        hr_ref[0, k1] = ar + br
        hi_ref[0, k1] = ai - bi
        return carry

    lax.fori_loop(0, nk, body, 0)


def _filter_feats(seq):
    t = np.linspace(0.0, 1.0, seq)[:, None]
    bands = (HY_EMB - 1) // 2
    f = np.linspace(1e-4, bands - 1, bands)[None, :]
    wpos = 2.0 * np.pi * np.arange(seq)[:, None] / seq
    z = np.concatenate([t, np.cos(f * wpos), -np.sin(f * wpos)], axis=-1)
    z = np.pad(z, ((0, 0), (0, LANES - HY_EMB)))
    tcol = np.repeat(t, LANES, axis=1)
    return jnp.asarray(z, F32), jnp.asarray(tcol, F32)


def _filter_spectra(seq, consts, fp):
    nb, n1, nt1, nk = _hy_dims(seq)
    ct = HY_CT
    nct = HY_W // ct
    z, tcol = _filter_feats(seq)
    smem = pl.BlockSpec(memory_space=pltpu.SMEM)
    per_layer = lambda r, c: pl.BlockSpec((1, r, c), lambda l, j: (l, 0, 0))
    out = pl.BlockSpec((1, nk, nb, ct), lambda l, j: (l, 0, 0, j))
    return pl.pallas_call(
        functools.partial(_filter_kernel, nb=nb, n1=n1, nk=nk),
        grid=(DEPTH, nct),
        in_specs=[
            smem, smem, _full((seq, LANES)), _full((seq, LANES)),
            per_layer(LANES, LANES), per_layer(1, LANES), per_layer(1, LANES),
            per_layer(LANES, LANES), per_layer(1, LANES),
            pl.BlockSpec((1, LANES, ct), lambda l, j: (l, 0, j)),
            pl.BlockSpec((1, LANES, ct), lambda l, j: (l, 0, nct + j)),
            pl.BlockSpec((1, 1, ct), lambda l, j: (l, 0, j)),
            pl.BlockSpec((1, 1, ct), lambda l, j: (l, 0, nct + j)),
            pl.BlockSpec((1, ct), lambda l, j: (0, j)),
            _full((nk, nb, LANES)), _full((nk, nb, LANES)), _full((2 * nb, 2 * nb)),
        ],
        out_specs=[out, out],
        out_shape=[jax.ShapeDtypeStruct((DEPTH, nk, nb, HY_W), F32)] * 2,
        scratch_shapes=[pltpu.VMEM((seq, ct), F32), pltpu.VMEM((seq, ct), F32)],
        compiler_params=pltpu.CompilerParams(
            dimension_semantics=("arbitrary", "arbitrary"), vmem_limit_bytes=VMEM_LIMIT),
        name="hyena_filter",
    )(consts["cos"], consts["sin"], z, tcol, fp["w1"], fp["b1"], fp["fq"], fp["w2"], fp["b2"],
      fp["w3"], fp["w3"], fp["b3"], fp["b3"], fp["deltas"], consts["twr"], consts["twi"], consts["fwd"])


def _rms(x, g):
    return x * lax.rsqrt(jnp.mean(x * x, axis=-1, keepdims=True) + EPS) * g


def _mlp_kernel(x_ref, of_ref, ob_ref, gate_ref, yh_ref, mod_ref, gng_ref, n2g_ref, fg_ref,
                wout_ref, w1_ref, w2_ref, o_ref, *, final):
    mod = mod_ref[0]
    gate1 = mod[:, 2 * D_MODEL:3 * D_MODEL]
    shift2 = mod[:, 3 * D_MODEL:4 * D_MODEL]
    scale2 = mod[:, 4 * D_MODEL:5 * D_MODEL]
    gate2 = mod[:, 5 * D_MODEL:6 * D_MODEL]
    o = of_ref[0, 0] + ob_ref[0, 0]
    gate = gate_ref[...]
    mix = jnp.dot(yh_ref[...].astype(BF16), wout_ref[GDN_W:, :], preferred_element_type=F32)
    for h in range(HEADS):
        hs = slice(h * DK, (h + 1) * DK)
        oh = _rms(o[:, hs], gng_ref[...]) * _silu(gate[:, hs])
        mix = mix + jnp.dot(oh.astype(BF16), wout_ref[hs, :], preferred_element_type=F32)
    x = x_ref[...] + gate1 * mix
    h2 = (_rms(x, n2g_ref[...]) * (1.0 + scale2) + shift2).astype(BF16)
    acc = None
    for cb in range(D_FF // D_MODEL):
        cs = slice(cb * D_MODEL, (cb + 1) * D_MODEL)
        a = jnp.maximum(jnp.dot(h2, w1_ref[:, cs], preferred_element_type=F32), 0.0)
        part = jnp.dot((a * a).astype(BF16), w2_ref[cs, :], preferred_element_type=F32)
        acc = part if acc is None else acc + part
    x = x + gate2 * acc
    if final:
        x = _rms(x, fg_ref[...])
    o_ref[...] = x


def _mlp(x, o, gate, yh, mod3, row0, tiles_per_row, tiles_per_seq, p, final):
    t = x.shape[0]
    tm = TOKEN_TILE
    tok = lambda w: pl.BlockSpec((tm, w), lambda i: (i, 0))

    def o_spec(d):
        return pl.BlockSpec((1, 1, tm, GDN_W), lambda i: (i // tiles_per_seq, d, i % tiles_per_seq, 0))

    return pl.pallas_call(
        functools.partial(_mlp_kernel, final=final),
        grid=(t // tm,),
        in_specs=[
            tok(D_MODEL), o_spec(0), o_spec(1), tok(GDN_W), tok(HY_W),
            pl.BlockSpec((1, 1, 6 * D_MODEL), lambda i: (row0 + i // tiles_per_row, 0, 0)),
            _full((1, DK)), _full((1, D_MODEL)), _full((1, D_MODEL)),
            _resident((D_MODEL, D_MODEL)), _resident((D_MODEL, D_FF)), _resident((D_FF, D_MODEL)),
        ],
        out_specs=tok(D_MODEL),
        out_shape=jax.ShapeDtypeStruct((t, D_MODEL), F32),
        compiler_params=pltpu.CompilerParams(
            dimension_semantics=("parallel",), vmem_limit_bytes=VMEM_LIMIT),
        name="mlp",
    )(x, o, o, gate, yh, mod3, p["gng"], p["n2g"], p["fg"], p["wout"], p["w1"], p["w2"])


def _pad_to(a, shape):
    return jnp.pad(a, [(0, s - d) for d, s in zip(a.shape, shape)])


def kernel(x_prompt, x_sample, state_gdn, c, c_ctx, w_ada, b_ada, norm1_g, norm2_g, w_in, gdn_conv_w, gdn_a_log, gdn_dt_bias, gdn_norm_g, hy_conv_w, hy_w1, hy_b1, hy_freq, hy_w2, hy_b2, hy_w3, hy_b3, hy_skip, w_out, w_mlp1, w_mlp2, final_g):
    nb_ctx, seq_ctx, _ = x_prompt.shape
    nb_smp, seq_smp, _ = x_sample.shape
    ctx_row = nb_smp

    cvec = _pad_to(jnp.concatenate([c, c_ctx[None]], axis=0), (MOD_ROWS, D_MODEL))
    mod = _modulation(cvec, w_ada, b_ada)

    layers = []
    for l in range(DEPTH):
        layers.append(dict(
            n1g=norm1_g[l][None], n2g=norm2_g[l][None], gng=gdn_norm_g[l][None], fg=final_g[None],
            wq=w_in[l][:, :OFF_A].astype(BF16),
            wab=_pad_to(w_in[l][:, OFF_A:OFF_HY], (D_MODEL, LANES)).astype(BF16),
            why=w_in[l][:, OFF_HY:].astype(BF16),
            cw=gdn_conv_w[l], hcw=hy_conv_w[l],
            alog=_pad_to(gdn_a_log[l].reshape(1, -1), (1, LANES)),
            dtb=_pad_to(gdn_dt_bias[l].reshape(1, -1), (1, LANES)),
            skip=hy_skip[l][None],
            wout=w_out[l].astype(BF16), w1=w_mlp1[l].astype(BF16), w2=w_mlp2[l].astype(BF16),
        ))

    max_decay = math.log(1e-2) / 0.3
    min_decay = math.log(1e-2) / 1.5
    fparams = dict(
        w1=_pad_to(hy_w1, (DEPTH, LANES, LANES)), b1=_pad_to(hy_b1[:, None], (DEPTH, 1, LANES)),
        fq=_pad_to(hy_freq[:, None], (DEPTH, 1, LANES)),
        w2=_pad_to(hy_w2, (DEPTH, LANES, LANES)), b2=_pad_to(hy_b2[:, None], (DEPTH, 1, LANES)),
        w3=_pad_to(hy_w3, (DEPTH, LANES, N_DIR * HY_W)), b3=hy_b3[:, None],
        deltas=jnp.asarray(np.abs(np.linspace(min_decay, max_decay, HY_W))[None], F32),
    )

    def run_group(x3, s0_of_layer, row0, per_batch_mod, seg):
        nb, seq, _ = x3.shape
        consts = _hy_consts(seq)
        spec_r, spec_i = _filter_spectra(seq, consts, fparams)
        x = x3.reshape(nb * seq, D_MODEL)
        tiles_per_seq = seq // TOKEN_TILE
        tiles_per_row = tiles_per_seq if per_batch_mod else nb * tiles_per_seq
        states = []
        for l in range(DEPTH):
            p = layers[l]
            mod3 = mod[l].reshape(MOD_ROWS, 1, 6 * D_MODEL)
            qkv, gate, gb, x0, zz = _proj(x, mod3, row0, tiles_per_row, seg, p)
            o, s_fin = _gdn(qkv, gb, s0_of_layer(l), nb, seq)
            yh = _hyena(zz, x0, p["skip"], spec_r[l], spec_i[l], consts, nb, seq)
            x = _mlp(x, o, gate, yh, mod3, row0, tiles_per_row, tiles_per_seq, p, l == DEPTH - 1)
            states.append(s_fin)
        return x.reshape(nb, seq, D_MODEL), states

    zero_state = jnp.zeros((nb_ctx, N_DIR, HEADS, DK, DK), F32)
    y_prompt, ctx_states = run_group(x_prompt, lambda l: zero_state, ctx_row, False, seq_ctx)
    new_state = jnp.stack(ctx_states, axis=1).astype(x_prompt.dtype)
    y_sample, _ = run_group(x_sample, lambda l: state_gdn[:, l].astype(F32), 0, True, GRID_W)
    return (y_prompt, y_sample, new_state)
```

```python
import functools
import math

import numpy as np
import jax
import jax.numpy as jnp
from jax import lax
from jax.experimental import pallas as pl
from jax.experimental.pallas import tpu as pltpu

F32 = jnp.float32
BF16 = jnp.bfloat16

D_MODEL = 1024
DEPTH = 4
GRID_W = 64
N_DIR = 2
HEADS = 4
DK = 128
GDN_W = HEADS * DK
GDN_CONV = 5
HY_W = D_MODEL - GDN_W
HY_CONV = 3
HY_EMB = 33
HY_FH = 64
D_FF = 4 * D_MODEL
EPS = 1e-6
OFF_G = 3 * GDN_W
OFF_A = 4 * GDN_W
OFF_B = OFF_A + N_DIR * HEADS
OFF_HY = OFF_B + N_DIR * HEADS
IN_COLS = OFF_HY + 3 * HY_W

LANES = 128
HALO = 8
CHUNK = 64
PACK = HEADS * CHUNK
GDN_GROUP = 8
TOKEN_TILE = 256
GDN_BLOCK = 1024
HY_BLOCK = 512
HY_CT = 256
FILTER_CT = 128
MOD_ROWS = 8
VMEM_LIMIT = 56 * 1024 * 1024


def _dot(a, b):
    return jnp.dot(a.astype(BF16), b.astype(BF16), preferred_element_type=F32)


def _dot_nt(a, b):
    return lax.dot_general(a.astype(BF16), b.astype(BF16), (((1,), (1,)), ((), ())),
                           preferred_element_type=F32)


def _dot_tn(a, b):
    return lax.dot_general(a.astype(BF16), b.astype(BF16), (((0,), (0,)), ((), ())),
                           preferred_element_type=F32)


def _split2(x):
    hi = x.astype(BF16)
    lo = (x - hi.astype(F32)).astype(BF16)
    return hi, lo


def _split3(x):
    hi = x.astype(BF16)
    r = x - hi.astype(F32)
    mid = r.astype(BF16)
    lo = (r - mid.astype(F32)).astype(BF16)
    return hi, mid, lo


def _dot_hp(a, b):
    ah, al = _split2(a)
    bh, bl = _split2(b)
    return (jnp.dot(ah, bh, preferred_element_type=F32)
            + (jnp.dot(ah, bl, preferred_element_type=F32)
               + jnp.dot(al, bh, preferred_element_type=F32)))


def _dot_mask(m, x):
    hi, mid, lo = _split3(x)
    return (jnp.dot(m, hi, preferred_element_type=F32)
            + (jnp.dot(m, mid, preferred_element_type=F32)
               + jnp.dot(m, lo, preferred_element_type=F32)))


def _silu(y):
    return y / (1.0 + jnp.exp(-y))


def _sigmoid(y):
    return 1.0 / (1.0 + jnp.exp(-y))


def _full(shape):
    nd = len(shape)
    return pl.BlockSpec(shape, lambda *_: (0,) * nd)


def _resident(shape):
    nd = len(shape)
    return pl.BlockSpec(shape, lambda *_: (0,) * nd, pipeline_mode=pl.Buffered(1))


def _mod_kernel(c_ref, w_ref, b_ref, o_ref):
    c = c_ref[...]
    o_ref[0] = _dot_hp(_silu(c), w_ref[0]) + b_ref[0]


def _modulation(cvec, w_ada, b_ada):
    tn = 1536
    n6 = 6 * D_MODEL
    return pl.pallas_call(
        _mod_kernel,
        grid=(DEPTH, n6 // tn),
        in_specs=[
            pl.BlockSpec((MOD_ROWS, D_MODEL), lambda l, j: (0, 0)),
            pl.BlockSpec((1, D_MODEL, tn), lambda l, j: (l, 0, j)),
            pl.BlockSpec((1, 1, tn), lambda l, j: (l, 0, j)),
        ],
        out_specs=pl.BlockSpec((1, MOD_ROWS, tn), lambda l, j: (l, 0, j)),
        out_shape=jax.ShapeDtypeStruct((DEPTH, MOD_ROWS, n6), F32),
        compiler_params=pltpu.CompilerParams(
            dimension_semantics=("arbitrary", "arbitrary"), vmem_limit_bytes=VMEM_LIMIT),
        name="modulation",
    )(cvec, w_ada, b_ada.reshape(DEPTH, 1, n6))


def _store_segments(buf, p, seg):
    st = seg + HALO
    for cb in range(p.shape[1] // LANES):
        for s in range(p.shape[0] // seg):
            buf[cb, s * st + HALO:s * st + HALO + seg, :] = p[s * seg:(s + 1) * seg, cb * LANES:(cb + 1) * LANES]


def _seg_conv(buf, bcol, w_ref, col0, width, seg, rows):
    pad = width // 2
    st = seg + HALO
    outs = []
    for s in range(rows // seg):
        acc = None
        for j in range(width):
            lo = s * st + HALO + j - pad
            term = buf[bcol // LANES, lo:lo + seg, :] * w_ref[j:j + 1, col0:col0 + LANES]
            acc = term if acc is None else acc + term
        outs.append(acc)
    return outs[0] if len(outs) == 1 else jnp.concatenate(outs, axis=0)


def _proj_kernel(x_ref, mod_ref, n1g_ref, wq_ref, wab_ref, why_ref, cw_ref, hcw_ref, alog_ref, dtb_ref,
                 qkv_ref, gate_ref, gb_ref, x0_ref, zz_ref, halo_sc, *, seg):
    tm = x_ref.shape[0]
    x = x_ref[...]
    mod = mod_ref[0]
    shift = mod[:, 0:D_MODEL]
    scale = mod[:, D_MODEL:2 * D_MODEL]
    hn = x * lax.rsqrt(jnp.mean(x * x, axis=-1, keepdims=True) + EPS) * n1g_ref[...]
    hn = (hn * (1.0 + scale) + shift).astype(BF16)
    for k in range(halo_sc.shape[0]):
        for cb in range(halo_sc.shape[1]):
            for s in range(tm // seg + 1):
                halo_sc[k, cb, s * (seg + HALO):s * (seg + HALO) + HALO, :] = jnp.zeros((HALO, LANES), F32)

    for part in range(3):
        buf = halo_sc.at[part]
        _store_segments(buf, jnp.dot(hn, wq_ref[:, part * GDN_W:(part + 1) * GDN_W],
                                     preferred_element_type=F32), seg)
        for h in range(HEADS):
            c0 = part * GDN_W + h * DK
            y = _silu(_seg_conv(buf, h * DK, cw_ref, c0, GDN_CONV, seg, tm))
            if part < 2:
                y = y * lax.rsqrt(jnp.sum(y * y, axis=-1, keepdims=True) + EPS)
            if part == 0:
                y = y * (DK ** -0.5)
            qkv_ref[:, c0:c0 + DK] = y
    gate_ref[...] = jnp.dot(hn, wq_ref[:, OFF_G:OFF_A], preferred_element_type=F32)

    z = jnp.dot(hn, wab_ref[...], preferred_element_type=F32)
    za = z + dtb_ref[...]
    softplus = jnp.maximum(za, 0.0) + jnp.log1p(jnp.exp(-jnp.abs(za)))
    g = -jnp.exp(alog_ref[...]) * softplus
    lane = lax.broadcasted_iota(jnp.int32, (tm, LANES), 1)
    gb_ref[...] = jnp.where(lane < N_DIR * HEADS, g, _sigmoid(z))

    for part in range(3):
        _store_segments(halo_sc.at[part], jnp.dot(hn, why_ref[:, part * HY_W:(part + 1) * HY_W],
                                                  preferred_element_type=F32), seg)
    for cb in range(HY_W // LANES):
        sl = slice(cb * LANES, (cb + 1) * LANES)
        x0_ref[:, sl] = _seg_conv(halo_sc.at[0], cb * LANES, hcw_ref, cb * LANES, HY_CONV, seg, tm)
        x1 = _seg_conv(halo_sc.at[1], cb * LANES, hcw_ref, HY_W + cb * LANES, HY_CONV, seg, tm)
        hv = _seg_conv(halo_sc.at[2], cb * LANES, hcw_ref, 2 * HY_W + cb * LANES, HY_CONV, seg, tm)
        zz_ref[:, sl] = x1 * hv


def _proj(x, mod3, row0, tiles_per_row, seg, p):
    t = x.shape[0]
    tm = TOKEN_TILE
    tok = lambda w: pl.BlockSpec((tm, w), lambda i: (i, 0))
    return pl.pallas_call(
        functools.partial(_proj_kernel, seg=seg),
        grid=(t // tm,),
        in_specs=[
            tok(D_MODEL),
            pl.BlockSpec((1, 1, 6 * D_MODEL), lambda i: (row0 + i // tiles_per_row, 0, 0)),
            _full((1, D_MODEL)),
            _resident((D_MODEL, OFF_A)),
            _resident((D_MODEL, LANES)),
            _resident((D_MODEL, 3 * HY_W)),
            _full((GDN_CONV, 3 * GDN_W)),
            _full((HY_CONV, 3 * HY_W)),
            _full((1, LANES)),
            _full((1, LANES)),
        ],
        out_specs=[tok(3 * GDN_W), tok(GDN_W), tok(LANES), tok(HY_W), tok(HY_W)],
        out_shape=[
            jax.ShapeDtypeStruct((t, 3 * GDN_W), F32),
            jax.ShapeDtypeStruct((t, GDN_W), F32),
            jax.ShapeDtypeStruct((t, LANES), F32),
            jax.ShapeDtypeStruct((t, HY_W), F32),
            jax.ShapeDtypeStruct((t, HY_W), F32),
        ],
        scratch_shapes=[pltpu.VMEM((3, GDN_W // LANES, (tm // seg) * (seg + HALO) + HALO, LANES), F32)],
        compiler_params=pltpu.CompilerParams(
            dimension_semantics=("parallel",), vmem_limit_bytes=VMEM_LIMIT),
        name="proj",
    )(x, mod3, p["n1g"], p["wq"], p["wab"], p["why"], p["cw"], p["hcw"], p["alog"], p["dtb"])


def _blockdiag(y, masks):
    return jnp.concatenate([jnp.where(m, y, jnp.zeros_like(y)) for m in masks], axis=0)


def _pdot(xs, ys, masks):
    rhs = [_blockdiag(y.astype(BF16), masks) for y in ys]
    return [jnp.dot(x.astype(BF16), r, preferred_element_type=F32) for x, r in zip(xs, rhs)]


def _unit_tri_inverse(a, eye, m16, m32, m64, hm):
    add = lambda ts, ps: [t + p for t, p in zip(ts, ps)]
    sub = lambda ts, ps: [t - p for t, p in zip(ts, ps)]
    n = [-(x * m16) for x in a]
    t = [eye + x for x in n]
    n2 = _pdot(n, n, hm)
    t = add(t, _pdot(t, n2, hm))
    n4 = _pdot(n2, n2, hm)
    t = add(t, _pdot(t, n4, hm))
    n8 = _pdot(n4, n4, hm)
    t = add(t, _pdot(t, n8, hm))
    t = sub(t, _pdot(t, _pdot([x * m32 for x in a], t, hm), hm))
    t = sub(t, _pdot(t, _pdot([x * m64 for x in a], t, hm), hm))
    return t


def _gdn_kernel(q_ref, k_ref, v_ref, gb_ref, s0_ref, o_ref, sfin_ref,
                s_sc, gph_sc, gpl_sc, gwh_sc, gwl_sc, bp_sc, bw_sc, u_sc, w_sc, qd_sc, kd_sc, qk_sc, eg_sc,
                *, nch, group):
    d = pl.program_id(1)
    j = pl.program_id(2)
    c = CHUNK
    fwd = d == 0

    @pl.when(j == 0)
    def _():
        s_sc[...] = s0_ref[0, 0]

    row = lax.broadcasted_iota(jnp.int32, (c, PACK), 0)
    lane = lax.broadcasted_iota(jnp.int32, (c, PACK), 1)
    col = lane & (c - 1)
    r = jnp.where(fwd, row, col)
    cc = jnp.where(fwd, col, row)
    tri = r >= cc
    strict_f = (r > cc).astype(F32)
    eye = (row == col).astype(F32)
    m16 = ((row >> 4) == (col >> 4)).astype(F32)
    same32 = ((row >> 5) == (col >> 5)).astype(F32)
    m32 = same32 - m16
    m64 = 1.0 - same32
    hm = [(lane >> 6) == h for h in range(HEADS)]
    lane_w = lax.broadcasted_iota(jnp.int32, (c, GDN_W), 1)
    hw = [(lane_w >> 7) == h for h in range(HEADS)]
    lane_w2 = lax.broadcasted_iota(jnp.int32, (c, 2 * GDN_W), 1)
    hw2 = [((lane_w2 >> 7) & (HEADS - 1)) == h for h in range(HEADS)]
    r2 = lax.broadcasted_iota(jnp.int32, (c, c), 0)
    c2 = lax.broadcasted_iota(jnp.int32, (c, c), 1)
    tri_b = (jnp.where(fwd, r2, c2) >= jnp.where(fwd, c2, r2)).astype(F32).astype(BF16)
    cum_lhs = jnp.concatenate([tri_b, jnp.ones((c, c), BF16)], axis=0)

    gb_hi, gb_lo = _split2(gb_ref[...])

    def spread(x, width, shift, first_row):
        sel_r = lax.broadcasted_iota(jnp.int32, (LANES, width), 0)
        sel_h = lax.broadcasted_iota(jnp.int32, (LANES, width), 1) >> shift
        sel = (sel_r == first_row + sel_h).astype(F32).astype(BF16)
        return jnp.dot(x, sel, preferred_element_type=F32)

    gph_sc[...] = spread(gb_hi, PACK, 6, d * HEADS).astype(BF16)
    gpl_sc[...] = spread(gb_lo, PACK, 6, d * HEADS).astype(BF16)
    gwh_sc[...] = spread(gb_hi, GDN_W, 7, d * HEADS).astype(BF16)
    gwl_sc[...] = spread(gb_lo, GDN_W, 7, d * HEADS).astype(BF16)
    bp_sc[...] = (spread(gb_hi, PACK, 6, (N_DIR + d) * HEADS)
                  + spread(gb_lo, PACK, 6, (N_DIR + d) * HEADS))
    bw_sc[...] = (spread(gb_hi, GDN_W, 7, (N_DIR + d) * HEADS)
                  + spread(gb_lo, GDN_W, 7, (N_DIR + d) * HEADS))
    strict = r > cc
    zero_p = jnp.zeros((c, PACK), BF16)

    def cumsum(lhs, hi, lo):
        return jnp.dot(lhs, hi, preferred_element_type=F32) + jnp.dot(lhs, lo, preferred_element_type=F32)

    def precompute(n, carry):
        grp = range(group)
        rows = [pl.ds(pl.multiple_of((n * group + i) * c, c), c) for i in grp]
        kc = [k_ref[r_, :] for r_ in rows]
        qc = [q_ref[r_, :] for r_ in rows]
        e = [cumsum(tri_b, jnp.where(strict, gph_sc[r_, :], zero_p), jnp.where(strict, gpl_sc[r_, :], zero_p))
             for r_ in rows]
        kb = [x.astype(BF16) for x in kc]
        kq = [lax.dot_general(jnp.concatenate([kb[i], qc[i].astype(BF16)], axis=0), _blockdiag(kb[i], hw),
                              (((1,), (1,)), ((), ())), preferred_element_type=F32) for i in grp]
        decay = [jnp.where(tri, jnp.exp(x), 0.0) for x in e]
        a = [bp_sc[rows[i], :] * kq[i][:c] * decay[i] * strict_f for i in grp]
        t = _unit_tri_inverse(a, eye, m16, m32, m64, hm)
        cums = [cumsum(cum_lhs, gwh_sc[r_, :], gwl_sc[r_, :]) for r_ in rows]
        gc = [x[:c] for x in cums]
        gt = [x[c:] for x in cums]
        egc = [jnp.exp(x) for x in gc]
        bw = [bw_sc[r_, :] for r_ in rows]
        uw = _pdot(t, [jnp.concatenate([v_ref[rows[i], :] * bw[i], kc[i] * bw[i] * egc[i]], axis=1)
                       for i in grp], hw2)
        for i in grp:
            u_sc[rows[i], :] = uw[i][:, :GDN_W]
            w_sc[rows[i], :] = uw[i][:, GDN_W:].astype(BF16)
            qd_sc[rows[i], :] = (qc[i] * egc[i]).astype(BF16)
            kd_sc[rows[i], :] = (kc[i] * jnp.exp(gt[i] - gc[i])).astype(BF16)
            qk_sc[rows[i], :] = (kq[i][c:] * decay[i]).astype(BF16)
            eg_sc[pl.ds(pl.multiple_of((n * group + i) * 8, 8), 8), :] = jnp.exp(gt[i][0:8, :])
        return carry

    lax.fori_loop(0, nch // group, precompute, 0)

    def recur(n, carry):
        ne = jnp.where(fwd, n, nch - 1 - n)
        rows = pl.ds(pl.multiple_of(ne * c, c), c)
        v_new = []
        q_s = []
        for h in range(HEADS):
            hs = slice(h * DK, (h + 1) * DK)
            both = jnp.dot(jnp.concatenate([w_sc[rows, hs], qd_sc[rows, hs]], axis=0),
                           s_sc[h].astype(BF16), preferred_element_type=F32)
            v_new.append(u_sc[rows, hs] - both[:c])
            q_s.append(both[c:])
        vb = jnp.concatenate(v_new, axis=1).astype(BF16)
        o_ref[0, 0, rows, :] = jnp.concatenate(q_s, axis=1) + jnp.dot(
            qk_sc[rows, :], _blockdiag(vb, hw), preferred_element_type=F32)
        eg = eg_sc[pl.ds(pl.multiple_of(ne * 8, 8), 8), :][0:1, :]
        for h in range(HEADS):
            hs = slice(h * DK, (h + 1) * DK)
            s_sc[h] = s_sc[h] * eg[:, hs] + lax.dot_general(
                kd_sc[rows, hs], vb[:, hs], (((0,), (0,)), ((), ())), preferred_element_type=F32)
        return carry

    lax.fori_loop(0, nch, recur, 0)

    @pl.when(j == pl.num_programs(2) - 1)
    def _():
        sfin_ref[0, 0] = s_sc[...]


def _gdn(qkv, gb, s0, nb, seq):
    lb = min(seq, GDN_BLOCK)
    nblk = seq // lb
    nch = lb // CHUNK

    def blk(b, d, j):
        return b * nblk + j + d * (nblk - 1 - 2 * j)

    def qkv_spec(part):
        return pl.BlockSpec((lb, GDN_W), lambda b, d, j: (blk(b, d, j), part))

    state_spec = pl.BlockSpec((1, 1, HEADS, DK, DK), lambda b, d, j: (b, d, 0, 0, 0))
    return pl.pallas_call(
        functools.partial(_gdn_kernel, nch=nch, group=min(nch, GDN_GROUP)),
        grid=(nb, N_DIR, nblk),
        in_specs=[
            qkv_spec(0), qkv_spec(1), qkv_spec(2),
            pl.BlockSpec((lb, LANES), lambda b, d, j: (blk(b, d, j), 0)),
            state_spec,
        ],
        out_specs=[
            pl.BlockSpec((1, 1, lb, GDN_W), lambda b, d, j: (b, d, j + d * (nblk - 1 - 2 * j), 0)),
            state_spec,
        ],
        out_shape=[
            jax.ShapeDtypeStruct((nb, N_DIR, seq, GDN_W), F32),
            jax.ShapeDtypeStruct((nb, N_DIR, HEADS, DK, DK), F32),
        ],
        scratch_shapes=[
            pltpu.VMEM((HEADS, DK, DK), F32),
            pltpu.VMEM((lb, PACK), BF16),
            pltpu.VMEM((lb, PACK), BF16),
            pltpu.VMEM((lb, GDN_W), BF16),
            pltpu.VMEM((lb, GDN_W), BF16),
            pltpu.VMEM((lb, PACK), F32),
            pltpu.VMEM((lb, GDN_W), F32),
            pltpu.VMEM((lb, GDN_W), F32),
            pltpu.VMEM((lb, GDN_W), BF16),
            pltpu.VMEM((lb, GDN_W), BF16),
            pltpu.VMEM((lb, GDN_W), BF16),
            pltpu.VMEM((lb, PACK), BF16),
            pltpu.VMEM((nch * 8, GDN_W), F32),
        ],
        compiler_params=pltpu.CompilerParams(
            dimension_semantics=("parallel", "arbitrary", "arbitrary"), vmem_limit_bytes=VMEM_LIMIT),
        name="gdn",
    )(qkv, qkv, qkv, gb, s0)


def _hy_dims(seq):
    nb = min(HY_BLOCK, seq)
    n1 = 2 * seq // nb
    return nb, n1, n1 // 2, n1 // 2 + 1


def _hy_consts(seq):
    nb, n1, nt1, nk = _hy_dims(seq)
    n = 2 * seq
    k2 = np.arange(nb)[:, None]
    t2 = np.arange(nb)[None, :]
    ang = 2.0 * np.pi * ((k2 * t2) % nb) / nb
    fr, fi = np.cos(ang), -np.sin(ang)
    fwd = np.block([[fr, -fi], [fi, fr]])
    inv = np.block([[fr, fi], [-fi, fr]])
    k1 = np.arange(nk)[:, None]
    ang = 2.0 * np.pi * (k1 * np.arange(nb)[None, :]) / n
    twr = np.repeat(np.cos(ang)[:, :, None], LANES, axis=2)
    twi = np.repeat(-np.sin(ang)[:, :, None], LANES, axis=2)
    ang = 2.0 * np.pi * ((k1 * np.arange(n1)[None, :]) % n1) / n1
    return dict(
        fwd=jnp.asarray(fwd, F32).astype(BF16), inv=jnp.asarray(inv, F32).astype(BF16),
        twr=jnp.asarray(twr, F32), twi=jnp.asarray(twi, F32),
        cos=jnp.asarray(np.cos(ang), F32), sin=jnp.asarray(np.sin(ang), F32))


def _lane_tile(x, width):
    return x if width == LANES else jnp.concatenate([x] * (width // LANES), axis=1)


def _hy_forward(block, k1, cos_ref, sin_ref, twr_ref, twi_ref, fwd_ref, nt1, nb):
    cr = None
    ci = None
    for t1 in range(nt1):
        xb = block(t1)
        tr = cos_ref[k1, t1] * xb
        ti = sin_ref[k1, t1] * xb
        cr = tr if cr is None else cr + tr
        ci = ti if ci is None else ci + ti
    ci = -ci
    twr = _lane_tile(twr_ref[k1], cr.shape[1])
    twi = _lane_tile(twi_ref[k1], cr.shape[1])
    c2 = jnp.concatenate([cr * twr - ci * twi, cr * twi + ci * twr], axis=0).astype(BF16)
    y = jnp.dot(fwd_ref[...], c2, preferred_element_type=F32)
    return y[:nb], y[nb:]


def _hyena_kernel(cos_ref, sin_ref, zz_ref, x0_ref, skip_ref, hr_ref, hi_ref, twr_ref, twi_ref,
                  fwd_ref, inv_ref, o_ref, zc_sc, *, nb, n1, nt1, nk):
    zc_sc[...] = jnp.zeros_like(zc_sc)

    def body(k1, carry):
        yr, yi = _hy_forward(lambda t1: zz_ref[t1 * nb:(t1 + 1) * nb, :], k1,
                             cos_ref, sin_ref, twr_ref, twi_ref, fwd_ref, nt1, nb)
        hr = hr_ref[k1]
        hi = hi_ref[k1]
        z2 = jnp.concatenate([yr * hr - yi * hi, yr * hi + yi * hr], axis=0).astype(BF16)
        dd = jnp.dot(inv_ref[...], z2, preferred_element_type=F32)
        dr, di = dd[:nb], dd[nb:]
        twr = _lane_tile(twr_ref[k1], dr.shape[1])
        twi = _lane_tile(twi_ref[k1], dr.shape[1])
        er = dr * twr + di * twi
        ei = di * twr - dr * twi
        wt = jnp.where((k1 == 0) | (k1 == n1 // 2), 1.0, 2.0) / (n1 * nb)
        for t1 in range(nt1):
            rows = slice(t1 * nb, (t1 + 1) * nb)
            zc_sc[rows, :] += (wt * cos_ref[k1, t1]) * er - (wt * sin_ref[k1, t1]) * ei
        return carry

    lax.fori_loop(0, nk, body, 0, unroll=3 if nk % 3 == 0 else 1)
    zz = zz_ref[...]
    o_ref[...] = x0_ref[...] * (zc_sc[...] + zz * skip_ref[...])


def _hyena(zz, x0, skip, hr, hi, consts, nb_batch, seq):
    nb, n1, nt1, nk = _hy_dims(seq)
    ct = HY_CT
    nct = HY_W // ct
    smem = pl.BlockSpec(memory_space=pltpu.SMEM)
    tok = pl.BlockSpec((seq, ct), lambda c, b: (b, c))
    tok_in = pl.BlockSpec((seq, ct), lambda c, b: (b, c), pipeline_mode=pl.Buffered(1))
    spec = pl.BlockSpec((nk, nb, ct), lambda c, b: (0, 0, c), pipeline_mode=pl.Buffered(1))
    return pl.pallas_call(
        functools.partial(_hyena_kernel, nb=nb, n1=n1, nt1=nt1, nk=nk),
        grid=(nct, nb_batch),
        in_specs=[
            smem, smem, tok_in, tok_in,
            pl.BlockSpec((1, ct), lambda c, b: (0, c)),
            spec, spec,
            _resident((nk, nb, LANES)), _resident((nk, nb, LANES)),
            _resident((2 * nb, 2 * nb)), _resident((2 * nb, 2 * nb)),
        ],
        out_specs=tok,
        out_shape=jax.ShapeDtypeStruct((nb_batch * seq, HY_W), F32),
        scratch_shapes=[pltpu.VMEM((seq, ct), F32)],
        compiler_params=pltpu.CompilerParams(
            dimension_semantics=("parallel", "parallel"), vmem_limit_bytes=VMEM_LIMIT),
        name="hyena",
    )(consts["cos"], consts["sin"], zz, x0, skip, hr, hi, consts["twr"], consts["twi"],
      consts["fwd"], consts["inv"])


def _filter_kernel(cos_ref, sin_ref, z_ref, tcol_ref, w1_ref, b1_ref, fq_ref, w2_ref, b2_ref,
                   w3a_ref, w3b_ref, b3a_ref, b3b_ref, dl_ref, twr_ref, twi_ref, fwd_ref,
                   hr_ref, hi_ref, h0_sc, h1_sc, a2_sc, *, nb, n1, nk):
    @pl.when(pl.program_id(1) == 0)
    def _():
        fq = fq_ref[0]
        a1 = jnp.sin(fq * (_dot_hp(z_ref[...], w1_ref[0]) + b1_ref[0]))
        a2_sc[...] = jnp.sin(fq * (_dot_hp(a1, w2_ref[0]) + b2_ref[0]))

    a2 = a2_sc[...]
    win = jnp.exp(-tcol_ref[...] * dl_ref[...])
    h0_sc[...] = (_dot_hp(a2, w3a_ref[0]) + b3a_ref[0]) * win
    h1 = (_dot_hp(a2, w3b_ref[0]) + b3b_ref[0]) * win
    rowid = lax.broadcasted_iota(jnp.int32, h1.shape, 0)
    h1_sc[...] = jnp.where(rowid == 0, 0.0, h1)

    def body(k1, carry):
        ar, ai = _hy_forward(lambda t1: h0_sc[t1 * nb:(t1 + 1) * nb, :], k1,
                             cos_ref, sin_ref, twr_ref, twi_ref, fwd_ref, n1 // 2, nb)
        br, bi = _hy_forward(lambda t1: h1_sc[t1 * nb:(t1 + 1) * nb, :], k1,
                             cos_ref, sin_ref, twr_ref, twi_ref, fwd_ref, n1 // 2, nb)
        hr_ref[0, k1] = ar + br
        hi_ref[0, k1] = ai - bi
        return carry

    lax.fori_loop(0, nk, body, 0)


def _filter_feats(seq):
    t = np.linspace(0.0, 1.0, seq)[:, None]
    bands = (HY_EMB - 1) // 2
    f = np.linspace(1e-4, bands - 1, bands)[None, :]
    wpos = 2.0 * np.pi * np.arange(seq)[:, None] / seq
    z = np.concatenate([t, np.cos(f * wpos), -np.sin(f * wpos)], axis=-1)
    z = np.pad(z, ((0, 0), (0, LANES - HY_EMB)))
    tcol = np.repeat(t, LANES, axis=1)
    return jnp.asarray(z, F32), jnp.asarray(tcol, F32)


def _filter_spectra(seq, consts, fp):
    nb, n1, nt1, nk = _hy_dims(seq)
    ct = FILTER_CT
    nct = HY_W // ct
    z, tcol = _filter_feats(seq)
    smem = pl.BlockSpec(memory_space=pltpu.SMEM)
    per_layer = lambda r, c: pl.BlockSpec((1, r, c), lambda l, j: (l, 0, 0))
    out = pl.BlockSpec((1, nk, nb, ct), lambda l, j: (l, 0, 0, j))
    return pl.pallas_call(
        functools.partial(_filter_kernel, nb=nb, n1=n1, nk=nk),
        grid=(DEPTH, nct),
        in_specs=[
            smem, smem, _full((seq, LANES)), _full((seq, LANES)),
            per_layer(LANES, LANES), per_layer(1, LANES), per_layer(1, LANES),
            per_layer(LANES, LANES), per_layer(1, LANES),
            pl.BlockSpec((1, LANES, ct), lambda l, j: (l, 0, j)),
            pl.BlockSpec((1, LANES, ct), lambda l, j: (l, 0, nct + j)),
            pl.BlockSpec((1, 1, ct), lambda l, j: (l, 0, j)),
            pl.BlockSpec((1, 1, ct), lambda l, j: (l, 0, nct + j)),
            pl.BlockSpec((1, ct), lambda l, j: (0, j)),
            _full((nk, nb, LANES)), _full((nk, nb, LANES)), _full((2 * nb, 2 * nb)),
        ],
        out_specs=[out, out],
        out_shape=[jax.ShapeDtypeStruct((DEPTH, nk, nb, HY_W), F32)] * 2,
        scratch_shapes=[pltpu.VMEM((seq, ct), F32), pltpu.VMEM((seq, ct), F32),
                        pltpu.VMEM((seq, LANES), F32)],
        compiler_params=pltpu.CompilerParams(
            dimension_semantics=("arbitrary", "arbitrary"), vmem_limit_bytes=VMEM_LIMIT),
        name="hyena_filter",
    )(consts["cos"], consts["sin"], z, tcol, fp["w1"], fp["b1"], fp["fq"], fp["w2"], fp["b2"],
      fp["w3"], fp["w3"], fp["b3"], fp["b3"], fp["deltas"], consts["twr"], consts["twi"], consts["fwd"])


def _rms(x, g):
    return x * lax.rsqrt(jnp.mean(x * x, axis=-1, keepdims=True) + EPS) * g


def _mlp_kernel(x_ref, of_ref, ob_ref, gate_ref, yh_ref, mod_ref, gng_ref, n2g_ref, fg_ref,
                wout_ref, w1_ref, w2_ref, o_ref, *, final):
    mod = mod_ref[0]
    gate1 = mod[:, 2 * D_MODEL:3 * D_MODEL]
    shift2 = mod[:, 3 * D_MODEL:4 * D_MODEL]
    scale2 = mod[:, 4 * D_MODEL:5 * D_MODEL]
    gate2 = mod[:, 5 * D_MODEL:6 * D_MODEL]
    o = of_ref[0, 0] + ob_ref[0, 0]
    gate = gate_ref[...]
    mix = jnp.dot(yh_ref[...].astype(BF16), wout_ref[GDN_W:, :], preferred_element_type=F32)
    for h in range(HEADS):
        hs = slice(h * DK, (h + 1) * DK)
        oh = _rms(o[:, hs], gng_ref[...]) * _silu(gate[:, hs])
        mix = mix + jnp.dot(oh.astype(BF16), wout_ref[hs, :], preferred_element_type=F32)
    x = x_ref[...] + gate1 * mix
    h2 = (_rms(x, n2g_ref[...]) * (1.0 + scale2) + shift2).astype(BF16)
    acc = None
    for cb in range(D_FF // D_MODEL):
        cs = slice(cb * D_MODEL, (cb + 1) * D_MODEL)
        a = jnp.maximum(jnp.dot(h2, w1_ref[:, cs], preferred_element_type=F32), 0.0)
        part = jnp.dot((a * a).astype(BF16), w2_ref[cs, :], preferred_element_type=F32)
        acc = part if acc is None else acc + part
    x = x + gate2 * acc
    if final:
        x = _rms(x, fg_ref[...])
    o_ref[...] = x


def _mlp(x, o, gate, yh, mod3, row0, tiles_per_row, tiles_per_seq, p, final):
    t = x.shape[0]
    tm = TOKEN_TILE
    tok = lambda w: pl.BlockSpec((tm, w), lambda i: (i, 0))

    def o_spec(d):
        return pl.BlockSpec((1, 1, tm, GDN_W), lambda i: (i // tiles_per_seq, d, i % tiles_per_seq, 0))

    return pl.pallas_call(
        functools.partial(_mlp_kernel, final=final),
        grid=(t // tm,),
        in_specs=[
            tok(D_MODEL), o_spec(0), o_spec(1), tok(GDN_W), tok(HY_W),
            pl.BlockSpec((1, 1, 6 * D_MODEL), lambda i: (row0 + i // tiles_per_row, 0, 0)),
            _full((1, DK)), _full((1, D_MODEL)), _full((1, D_MODEL)),
            _resident((D_MODEL, D_MODEL)), _resident((D_MODEL, D_FF)), _resident((D_FF, D_MODEL)),
        ],
        out_specs=tok(D_MODEL),
        out_shape=jax.ShapeDtypeStruct((t, D_MODEL), F32),
        compiler_params=pltpu.CompilerParams(
            dimension_semantics=("parallel",), vmem_limit_bytes=VMEM_LIMIT),
        name="mlp",
    )(x, o, o, gate, yh, mod3, p["gng"], p["n2g"], p["fg"], p["wout"], p["w1"], p["w2"])


def _pad_to(a, shape):
    return jnp.pad(a, [(0, s - d) for d, s in zip(a.shape, shape)])


def kernel(x_prompt, x_sample, state_gdn, c, c_ctx, w_ada, b_ada, norm1_g, norm2_g, w_in, gdn_conv_w, gdn_a_log, gdn_dt_bias, gdn_norm_g, hy_conv_w, hy_w1, hy_b1, hy_freq, hy_w2, hy_b2, hy_w3, hy_b3, hy_skip, w_out, w_mlp1, w_mlp2, final_g):
    nb_ctx, seq_ctx, _ = x_prompt.shape
    nb_smp, seq_smp, _ = x_sample.shape
    ctx_row = nb_smp

    cvec = _pad_to(jnp.concatenate([c, c_ctx[None]], axis=0), (MOD_ROWS, D_MODEL))
    mod = _modulation(cvec, w_ada, b_ada)

    layers = []
    for l in range(DEPTH):
        layers.append(dict(
            n1g=norm1_g[l][None], n2g=norm2_g[l][None], gng=gdn_norm_g[l][None], fg=final_g[None],
            wq=w_in[l][:, :OFF_A].astype(BF16),
            wab=_pad_to(w_in[l][:, OFF_A:OFF_HY], (D_MODEL, LANES)).astype(BF16),
            why=w_in[l][:, OFF_HY:].astype(BF16),
            cw=gdn_conv_w[l], hcw=hy_conv_w[l],
            alog=_pad_to(gdn_a_log[l].reshape(1, -1), (1, LANES)),
            dtb=_pad_to(gdn_dt_bias[l].reshape(1, -1), (1, LANES)),
            skip=hy_skip[l][None],
            wout=w_out[l].astype(BF16), w1=w_mlp1[l].astype(BF16), w2=w_mlp2[l].astype(BF16),
        ))

    max_decay = math.log(1e-2) / 0.3
    min_decay = math.log(1e-2) / 1.5
    fparams = dict(
        w1=_pad_to(hy_w1, (DEPTH, LANES, LANES)), b1=_pad_to(hy_b1[:, None], (DEPTH, 1, LANES)),
        fq=_pad_to(hy_freq[:, None], (DEPTH, 1, LANES)),
        w2=_pad_to(hy_w2, (DEPTH, LANES, LANES)), b2=_pad_to(hy_b2[:, None], (DEPTH, 1, LANES)),
        w3=_pad_to(hy_w3, (DEPTH, LANES, N_DIR * HY_W)), b3=hy_b3[:, None],
        deltas=jnp.asarray(np.abs(np.linspace(min_decay, max_decay, HY_W))[None], F32),
    )

    def run_group(x3, s0_of_layer, row0, per_batch_mod, seg):
        nb, seq, _ = x3.shape
        consts = _hy_consts(seq)
        spec_r, spec_i = _filter_spectra(seq, consts, fparams)
        x = x3.reshape(nb * seq, D_MODEL)
        tiles_per_seq = seq // TOKEN_TILE
        tiles_per_row = tiles_per_seq if per_batch_mod else nb * tiles_per_seq
        states = []
        for l in range(DEPTH):
            p = layers[l]
            mod3 = mod[l].reshape(MOD_ROWS, 1, 6 * D_MODEL)
            qkv, gate, gb, x0, zz = _proj(x, mod3, row0, tiles_per_row, seg, p)
            o, s_fin = _gdn(qkv, gb, s0_of_layer(l), nb, seq)
            yh = _hyena(zz, x0, p["skip"], spec_r[l], spec_i[l], consts, nb, seq)
            x = _mlp(x, o, gate, yh, mod3, row0, tiles_per_row, tiles_per_seq, p, l == DEPTH - 1)
            states.append(s_fin)
        return x.reshape(nb, seq, D_MODEL), states

    zero_state = jnp.zeros((nb_ctx, N_DIR, HEADS, DK, DK), F32)
    y_prompt, ctx_states = run_group(x_prompt, lambda l: zero_state, ctx_row, False, seq_ctx)
    new_state = jnp.stack(ctx_states, axis=1).astype(x_prompt.dtype)
    y_sample, _ = run_group(x_sample, lambda l: state_gdn[:, l].astype(F32), 0, True, GRID_W)
    return (y_prompt, y_sample, new_state)
```

```python
import functools
import math

import numpy as np
import jax
import jax.numpy as jnp
from jax import lax
from jax.experimental import pallas as pl
from jax.experimental.pallas import tpu as pltpu

F32 = jnp.float32
BF16 = jnp.bfloat16

D_MODEL = 1024
DEPTH = 4
GRID_W = 64
N_DIR = 2
HEADS = 4
DK = 128
GDN_W = HEADS * DK
GDN_CONV = 5
HY_W = D_MODEL - GDN_W
HY_CONV = 3
HY_EMB = 33
HY_FH = 64
D_FF = 4 * D_MODEL
EPS = 1e-6
OFF_G = 3 * GDN_W
OFF_A = 4 * GDN_W
OFF_B = OFF_A + N_DIR * HEADS
OFF_HY = OFF_B + N_DIR * HEADS
IN_COLS = OFF_HY + 3 * HY_W

LANES = 128
HALO = 8
CHUNK = 64
PACK = HEADS * CHUNK
GDN_GROUP = 8
TOKEN_TILE = 256
MLP_TILE = 512
GDN_BLOCK = 256
GDN_BATCH = 4
HY_BLOCK = 512
HY_CT = 256
FILTER_CT = 128
HY_ROWS = 4096
HY_SEQS = 8
MOD_ROWS = 8
VMEM_LIMIT = 56 * 1024 * 1024


def _split2(x):
    hi = x.astype(BF16)
    lo = (x - hi.astype(F32)).astype(BF16)
    return hi, lo


def _dot_hp(a, b):
    ah, al = _split2(a)
    bh, bl = _split2(b)
    return (jnp.dot(ah, bh, preferred_element_type=F32)
            + (jnp.dot(ah, bl, preferred_element_type=F32)
               + jnp.dot(al, bh, preferred_element_type=F32)))


def _silu(y):
    return y / (1.0 + jnp.exp(-y))


def _sigmoid(y):
    return 1.0 / (1.0 + jnp.exp(-y))


def _full(shape):
    nd = len(shape)
    return pl.BlockSpec(shape, lambda *_: (0,) * nd)


def _resident(shape):
    nd = len(shape)
    return pl.BlockSpec(shape, lambda *_: (0,) * nd, pipeline_mode=pl.Buffered(1))


def _mod_kernel(c_ref, w_ref, b_ref, o_ref):
    c = c_ref[...]
    o_ref[0] = _dot_hp(_silu(c), w_ref[0]) + b_ref[0]


def _modulation(cvec, w_ada, b_ada):
    tn = 1536
    n6 = 6 * D_MODEL
    return pl.pallas_call(
        _mod_kernel,
        grid=(DEPTH, n6 // tn),
        in_specs=[
            pl.BlockSpec((MOD_ROWS, D_MODEL), lambda l, j: (0, 0)),
            pl.BlockSpec((1, D_MODEL, tn), lambda l, j: (l, 0, j)),
            pl.BlockSpec((1, 1, tn), lambda l, j: (l, 0, j)),
        ],
        out_specs=pl.BlockSpec((1, MOD_ROWS, tn), lambda l, j: (l, 0, j)),
        out_shape=jax.ShapeDtypeStruct((DEPTH, MOD_ROWS, n6), F32),
        compiler_params=pltpu.CompilerParams(
            dimension_semantics=("arbitrary", "arbitrary"), vmem_limit_bytes=VMEM_LIMIT),
        name="modulation",
    )(cvec, w_ada, b_ada.reshape(DEPTH, 1, n6))


def _store_segments(buf, p, seg):
    st = seg + HALO
    for cb in range(p.shape[1] // LANES):
        for s in range(p.shape[0] // seg):
            buf[cb, s * st + HALO:s * st + HALO + seg, :] = p[s * seg:(s + 1) * seg, cb * LANES:(cb + 1) * LANES]


def _seg_conv(buf, bcol, w_ref, col0, width, seg, rows):
    pad = width // 2
    st = seg + HALO
    outs = []
    for s in range(rows // seg):
        acc = None
        for j in range(width):
            lo = s * st + HALO + j - pad
            term = buf[bcol // LANES, lo:lo + seg, :] * w_ref[j:j + 1, col0:col0 + LANES]
            acc = term if acc is None else acc + term
        outs.append(acc)
    return outs[0] if len(outs) == 1 else jnp.concatenate(outs, axis=0)


def _proj_kernel(x_ref, mod_ref, n1g_ref, wq_ref, wab_ref, why_ref, cw_ref, hcw_ref, alog_ref, dtb_ref,
                 qkv_ref, gate_ref, gb_ref, x0_ref, zz_ref, halo_sc, *, seg):
    tm = x_ref.shape[0]
    x = x_ref[...]
    mod = mod_ref[0]
    shift = mod[:, 0:D_MODEL]
    scale = mod[:, D_MODEL:2 * D_MODEL]
    hn = x * lax.rsqrt(jnp.mean(x * x, axis=-1, keepdims=True) + EPS) * n1g_ref[...]
    hn = (hn * (1.0 + scale) + shift).astype(BF16)
    for k in range(halo_sc.shape[0]):
        for cb in range(halo_sc.shape[1]):
            for s in range(tm // seg + 1):
                halo_sc[k, cb, s * (seg + HALO):s * (seg + HALO) + HALO, :] = jnp.zeros((HALO, LANES), F32)

    for part in range(3):
        buf = halo_sc.at[part]
        _store_segments(buf, jnp.dot(hn, wq_ref[:, part * GDN_W:(part + 1) * GDN_W],
                                     preferred_element_type=F32), seg)
        for h in range(HEADS):
            c0 = part * GDN_W + h * DK
            y = _silu(_seg_conv(buf, h * DK, cw_ref, c0, GDN_CONV, seg, tm))
            if part < 2:
                y = y * lax.rsqrt(jnp.sum(y * y, axis=-1, keepdims=True) + EPS)
            if part == 0:
                y = y * (DK ** -0.5)
            qkv_ref[:, c0:c0 + DK] = y
    gate_ref[...] = jnp.dot(hn, wq_ref[:, OFF_G:OFF_A], preferred_element_type=F32)

    z = jnp.dot(hn, wab_ref[...], preferred_element_type=F32)
    za = z + dtb_ref[...]
    softplus = jnp.maximum(za, 0.0) + jnp.log1p(jnp.exp(-jnp.abs(za)))
    g = -jnp.exp(alog_ref[...]) * softplus
    lane = lax.broadcasted_iota(jnp.int32, (tm, LANES), 1)
    gb_ref[...] = jnp.where(lane < N_DIR * HEADS, g, _sigmoid(z))

    for part in range(3):
        _store_segments(halo_sc.at[part], jnp.dot(hn, why_ref[:, part * HY_W:(part + 1) * HY_W],
                                                  preferred_element_type=F32), seg)
    for cb in range(HY_W // LANES):
        sl = slice(cb * LANES, (cb + 1) * LANES)
        x0_ref[:, sl] = _seg_conv(halo_sc.at[0], cb * LANES, hcw_ref, cb * LANES, HY_CONV, seg, tm)
        x1 = _seg_conv(halo_sc.at[1], cb * LANES, hcw_ref, HY_W + cb * LANES, HY_CONV, seg, tm)
        hv = _seg_conv(halo_sc.at[2], cb * LANES, hcw_ref, 2 * HY_W + cb * LANES, HY_CONV, seg, tm)
        zz_ref[:, sl] = x1 * hv


def _proj(x, mod3, row0, tiles_per_row, seg, p):
    t = x.shape[0]
    tm = TOKEN_TILE
    tok = lambda w: pl.BlockSpec((tm, w), lambda i: (i, 0))
    return pl.pallas_call(
        functools.partial(_proj_kernel, seg=seg),
        grid=(t // tm,),
        in_specs=[
            tok(D_MODEL),
            pl.BlockSpec((1, 1, 6 * D_MODEL), lambda i: (row0 + i // tiles_per_row, 0, 0)),
            _full((1, D_MODEL)),
            _resident((D_MODEL, OFF_A)),
            _resident((D_MODEL, LANES)),
            _resident((D_MODEL, 3 * HY_W)),
            _full((GDN_CONV, 3 * GDN_W)),
            _full((HY_CONV, 3 * HY_W)),
            _full((1, LANES)),
            _full((1, LANES)),
        ],
        out_specs=[tok(3 * GDN_W), tok(GDN_W), tok(LANES), tok(HY_W), tok(HY_W)],
        out_shape=[
            jax.ShapeDtypeStruct((t, 3 * GDN_W), F32),
            jax.ShapeDtypeStruct((t, GDN_W), F32),
            jax.ShapeDtypeStruct((t, LANES), F32),
            jax.ShapeDtypeStruct((t, HY_W), F32),
            jax.ShapeDtypeStruct((t, HY_W), F32),
        ],
        scratch_shapes=[pltpu.VMEM((3, GDN_W // LANES, (tm // seg) * (seg + HALO) + HALO, LANES), F32)],
        compiler_params=pltpu.CompilerParams(
            dimension_semantics=("parallel",), vmem_limit_bytes=VMEM_LIMIT),
        name="proj",
    )(x, mod3, p["n1g"], p["wq"], p["wab"], p["why"], p["cw"], p["hcw"], p["alog"], p["dtb"])


def _aligned(x, m):
    return x if isinstance(x, int) else pl.multiple_of(x, m)


def _blockdiag(y, masks):
    return jnp.concatenate([jnp.where(m, y, jnp.zeros_like(y)) for m in masks], axis=0)


def _pdot(xs, ys, masks):
    rhs = [_blockdiag(y.astype(BF16), masks) for y in ys]
    return [jnp.dot(x.astype(BF16), r, preferred_element_type=F32) for x, r in zip(xs, rhs)]


def _unit_tri_inverse(a, eye, m16, m32, m64, hm):
    add = lambda ts, ps: [t + p for t, p in zip(ts, ps)]
    sub = lambda ts, ps: [t - p for t, p in zip(ts, ps)]
    n = [-(x * m16) for x in a]
    t = [eye + x for x in n]
    n2 = _pdot(n, n, hm)
    t = add(t, _pdot(t, n2, hm))
    n4 = _pdot(n2, n2, hm)
    t = add(t, _pdot(t, n4, hm))
    n8 = _pdot(n4, n4, hm)
    t = add(t, _pdot(t, n8, hm))
    t = sub(t, _pdot(t, _pdot([x * m32 for x in a], t, hm), hm))
    t = sub(t, _pdot(t, _pdot([x * m64 for x in a], t, hm), hm))
    return t


def _gdn_kernel(q_ref, k_ref, v_ref, gb_ref, s0_ref, o_ref, sfin_ref,
                s_sc, gph_sc, gpl_sc, gwh_sc, gwl_sc, bp_sc, bw_sc, u_sc, w_sc, qd_sc, kd_sc, qk_sc, eg_sc,
                *, nch, group):
    d = pl.program_id(1)
    j = pl.program_id(2)
    c = CHUNK
    fwd = d == 0
    nbb = q_ref.shape[0]

    @pl.when(j == 0)
    def _():
        s_sc[...] = s0_ref[:, 0]

    row = lax.broadcasted_iota(jnp.int32, (c, PACK), 0)
    lane = lax.broadcasted_iota(jnp.int32, (c, PACK), 1)
    col = lane & (c - 1)
    r = jnp.where(fwd, row, col)
    cc = jnp.where(fwd, col, row)
    tri = r >= cc
    strict_f = (r > cc).astype(F32)
    eye = (row == col).astype(F32)
    m16 = ((row >> 4) == (col >> 4)).astype(F32)
    same32 = ((row >> 5) == (col >> 5)).astype(F32)
    m32 = same32 - m16
    m64 = 1.0 - same32
    hm = [(lane >> 6) == h for h in range(HEADS)]
    lane_w = lax.broadcasted_iota(jnp.int32, (c, GDN_W), 1)
    hw = [(lane_w >> 7) == h for h in range(HEADS)]
    lane_w2 = lax.broadcasted_iota(jnp.int32, (c, 2 * GDN_W), 1)
    hw2 = [((lane_w2 >> 7) & (HEADS - 1)) == h for h in range(HEADS)]
    r2 = lax.broadcasted_iota(jnp.int32, (c, c), 0)
    c2 = lax.broadcasted_iota(jnp.int32, (c, c), 1)
    tri_b = (jnp.where(fwd, r2, c2) >= jnp.where(fwd, c2, r2)).astype(F32).astype(BF16)
    cum_lhs = jnp.concatenate([tri_b, jnp.ones((c, c), BF16)], axis=0)

    def selector(width, shift, first_row):
        sel_r = lax.broadcasted_iota(jnp.int32, (LANES, width), 0)
        sel_h = lax.broadcasted_iota(jnp.int32, (LANES, width), 1) >> shift
        return (sel_r == first_row + sel_h).astype(F32).astype(BF16)

    sel_gp = selector(PACK, 6, d * HEADS)
    sel_gw = selector(GDN_W, 7, d * HEADS)
    sel_bp = selector(PACK, 6, (N_DIR + d) * HEADS)
    sel_bw = selector(GDN_W, 7, (N_DIR + d) * HEADS)
    spread = lambda x, sel: jnp.dot(x, sel, preferred_element_type=F32)
    for bb in range(nbb):
        gb_hi, gb_lo = _split2(gb_ref[bb])
        gph_sc[bb] = spread(gb_hi, sel_gp).astype(BF16)
        gpl_sc[bb] = spread(gb_lo, sel_gp).astype(BF16)
        gwh_sc[bb] = spread(gb_hi, sel_gw).astype(BF16)
        gwl_sc[bb] = spread(gb_lo, sel_gw).astype(BF16)
        bp_sc[bb] = spread(gb_hi, sel_bp) + spread(gb_lo, sel_bp)
        bw_sc[bb] = spread(gb_hi, sel_bw) + spread(gb_lo, sel_bw)
    strict = r > cc
    zero_p = jnp.zeros((c, PACK), BF16)

    def cumsum(lhs, hi, lo):
        return jnp.dot(lhs, hi, preferred_element_type=F32) + jnp.dot(lhs, lo, preferred_element_type=F32)

    def precompute(it, carry):
        grp = range(group)
        if nch >= group:
            seq_of = [it // (nch // group)] * group
            chunk_of = [(it - seq_of[0] * (nch // group)) * group + i for i in grp]
        else:
            seq_of = [it * (group // nch) + i // nch for i in grp]
            chunk_of = [i % nch for i in grp]
        at = [(seq_of[i], pl.ds(_aligned(chunk_of[i] * c, c), c)) for i in grp]
        kc = [k_ref[bb, r_, :] for bb, r_ in at]
        qc = [q_ref[bb, r_, :] for bb, r_ in at]
        e = [cumsum(tri_b, jnp.where(strict, gph_sc[bb, r_, :], zero_p),
                    jnp.where(strict, gpl_sc[bb, r_, :], zero_p)) for bb, r_ in at]
        kb = [x.astype(BF16) for x in kc]
        kq = [lax.dot_general(jnp.concatenate([kb[i], qc[i].astype(BF16)], axis=0), _blockdiag(kb[i], hw),
                              (((1,), (1,)), ((), ())), preferred_element_type=F32) for i in grp]
        decay = [jnp.where(tri, jnp.exp(x), 0.0) for x in e]
        a = [bp_sc[at[i][0], at[i][1], :] * kq[i][:c] * decay[i] * strict_f for i in grp]
        t = _unit_tri_inverse(a, eye, m16, m32, m64, hm)
        cums = [cumsum(cum_lhs, gwh_sc[bb, r_, :], gwl_sc[bb, r_, :]) for bb, r_ in at]
        gc = [x[:c] for x in cums]
        gt = [x[c:] for x in cums]
        egc = [jnp.exp(x) for x in gc]
        bw = [bw_sc[bb, r_, :] for bb, r_ in at]
        uw = _pdot(t, [jnp.concatenate([v_ref[at[i][0], at[i][1], :] * bw[i], kc[i] * bw[i] * egc[i]], axis=1)
                       for i in grp], hw2)
        for i in grp:
            bb, r_ = at[i]
            u_sc[bb, r_, :] = uw[i][:, :GDN_W]
            w_sc[bb, r_, :] = uw[i][:, GDN_W:].astype(BF16)
            qd_sc[bb, r_, :] = (qc[i] * egc[i]).astype(BF16)
            kd_sc[bb, r_, :] = (kc[i] * jnp.exp(gt[i] - gc[i])).astype(BF16)
            qk_sc[bb, r_, :] = (kq[i][c:] * decay[i]).astype(BF16)
            eg_sc[bb, pl.ds(_aligned(chunk_of[i] * 8, 8), 8), :] = jnp.exp(gt[i][0:8, :])
        return carry

    lax.fori_loop(0, nbb * nch // group, precompute, 0)

    def recur(n, carry):
        ne = jnp.where(fwd, n, nch - 1 - n)
        rows = pl.ds(pl.multiple_of(ne * c, c), c)
        seqs = range(nbb)
        heads = [slice(h * DK, (h + 1) * DK) for h in range(HEADS)]
        s_old = [[s_sc[bb, h] for h in range(HEADS)] for bb in seqs]
        both = [[jnp.dot(jnp.concatenate([w_sc[bb, rows, hs], qd_sc[bb, rows, hs]], axis=0),
                         s_old[bb][h].astype(BF16), preferred_element_type=F32)
                 for h, hs in enumerate(heads)] for bb in seqs]
        vb = [jnp.concatenate([u_sc[bb, rows, hs] - both[bb][h][:c] for h, hs in enumerate(heads)],
                              axis=1).astype(BF16) for bb in seqs]
        upd = [[lax.dot_general(kd_sc[bb, rows, hs], vb[bb][:, hs], (((0,), (0,)), ((), ())),
                                preferred_element_type=F32) for hs in heads] for bb in seqs]
        for bb in seqs:
            eg = eg_sc[bb, pl.ds(pl.multiple_of(ne * 8, 8), 8), :][0:1, :]
            for h, hs in enumerate(heads):
                s_sc[bb, h] = s_old[bb][h] * eg[:, hs] + upd[bb][h]
        for bb in seqs:
            o_ref[bb, 0, rows, :] = jnp.concatenate([both[bb][h][c:] for h in range(HEADS)], axis=1) + jnp.dot(
                qk_sc[bb, rows, :], _blockdiag(vb[bb], hw), preferred_element_type=F32)
        return carry

    lax.fori_loop(0, nch, recur, 0)

    @pl.when(j == pl.num_programs(2) - 1)
    def _():
        sfin_ref[:, 0] = s_sc[...]


def _gdn(qkv, gb, s0, nb, seq):
    lb = min(seq, GDN_BLOCK)
    nblk = seq // lb
    nch = lb // CHUNK

    nbb = GDN_BATCH

    def blk(d, j):
        return j + d * (nblk - 1 - 2 * j)

    def qkv_spec(part):
        return pl.BlockSpec((nbb, lb, GDN_W), lambda b, d, j: (b, blk(d, j), part))

    state_spec = pl.BlockSpec((nbb, 1, HEADS, DK, DK), lambda b, d, j: (b, d, 0, 0, 0))
    qkv3 = qkv.reshape(nb, seq, 3 * GDN_W)
    return pl.pallas_call(
        functools.partial(_gdn_kernel, nch=nch, group=min(nbb * nch, GDN_GROUP)),
        grid=(nb // nbb, N_DIR, nblk),
        in_specs=[
            qkv_spec(0), qkv_spec(1), qkv_spec(2),
            pl.BlockSpec((nbb, lb, LANES), lambda b, d, j: (b, blk(d, j), 0)),
            state_spec,
        ],
        out_specs=[
            pl.BlockSpec((nbb, 1, lb, GDN_W), lambda b, d, j: (b, d, blk(d, j), 0)),
            state_spec,
        ],
        out_shape=[
            jax.ShapeDtypeStruct((nb, N_DIR, seq, GDN_W), F32),
            jax.ShapeDtypeStruct((nb, N_DIR, HEADS, DK, DK), F32),
        ],
        scratch_shapes=[
            pltpu.VMEM((nbb, HEADS, DK, DK), F32),
            pltpu.VMEM((nbb, lb, PACK), BF16),
            pltpu.VMEM((nbb, lb, PACK), BF16),
            pltpu.VMEM((nbb, lb, GDN_W), BF16),
            pltpu.VMEM((nbb, lb, GDN_W), BF16),
            pltpu.VMEM((nbb, lb, PACK), F32),
            pltpu.VMEM((nbb, lb, GDN_W), F32),
            pltpu.VMEM((nbb, lb, GDN_W), F32),
            pltpu.VMEM((nbb, lb, GDN_W), BF16),
            pltpu.VMEM((nbb, lb, GDN_W), BF16),
            pltpu.VMEM((nbb, lb, GDN_W), BF16),
            pltpu.VMEM((nbb, lb, PACK), BF16),
            pltpu.VMEM((nbb, nch * 8, GDN_W), F32),
        ],
        compiler_params=pltpu.CompilerParams(
            dimension_semantics=("parallel", "arbitrary", "arbitrary"), vmem_limit_bytes=VMEM_LIMIT),
        name="gdn",
    )(qkv3, qkv3, qkv3, gb.reshape(nb, seq, LANES), s0)


def _hy_dims(seq):
    nb = min(HY_BLOCK, seq)
    n1 = 2 * seq // nb
    return nb, n1, n1 // 2, n1 // 2 + 1


def _hy_consts(seq):
    nb, n1, nt1, nk = _hy_dims(seq)
    n = 2 * seq
    k2 = np.arange(nb)[:, None]
    t2 = np.arange(nb)[None, :]
    ang = 2.0 * np.pi * ((k2 * t2) % nb) / nb
    fr, fi = np.cos(ang), -np.sin(ang)
    fwd = np.block([[fr, -fi], [fi, fr]])
    inv = np.block([[fr, fi], [-fi, fr]])
    k1 = np.arange(nk)[:, None]
    ang = 2.0 * np.pi * (k1 * np.arange(nb)[None, :]) / n
    twr = np.repeat(np.cos(ang)[:, :, None], LANES, axis=2)
    twi = np.repeat(-np.sin(ang)[:, :, None], LANES, axis=2)
    ang = 2.0 * np.pi * ((k1 * np.arange(n1)[None, :]) % n1) / n1
    return dict(
        fwd=jnp.asarray(fwd, F32).astype(BF16), inv=jnp.asarray(inv, F32).astype(BF16),
        twr=jnp.asarray(twr, F32), twi=jnp.asarray(twi, F32),
        cos=jnp.asarray(np.cos(ang), F32), sin=jnp.asarray(np.sin(ang), F32))


def _lane_tile(x, width):
    return x if width == LANES else jnp.concatenate([x] * (width // LANES), axis=1)


def _hy_forward(blocks, k1, cos_ref, sin_ref, twr_ref, twi_ref, fwd_ref, nt1, nb):
    cs = []
    for block in blocks:
        cr = None
        ci = None
        for t1 in range(nt1):
            xb = block(t1)
            tr = cos_ref[k1, t1] * xb
            ti = sin_ref[k1, t1] * xb
            cr = tr if cr is None else cr + tr
            ci = ti if ci is None else ci + ti
        cs.append((cr, -ci))
    width = cs[0][0].shape[1]
    twr = _lane_tile(twr_ref[k1], width)
    twi = _lane_tile(twi_ref[k1], width)
    c2 = [jnp.concatenate([cr * twr - ci * twi, cr * twi + ci * twr], axis=0).astype(BF16) for cr, ci in cs]
    ys = [jnp.dot(fwd_ref[...], x, preferred_element_type=F32) for x in c2]
    return [(y[:nb], y[nb:]) for y in ys]


def _hyena_kernel(cos_ref, sin_ref, zz_ref, x0_ref, skip_ref, hr_ref, hi_ref, twr_ref, twi_ref,
                  fwd_ref, inv_ref, o_ref, zc_sc, *, nb, n1, nt1, nk, seq, sps):
    zc_sc[...] = jnp.zeros_like(zc_sc)

    def body(k1, carry):
        ys = _hy_forward([lambda t1, s=s: zz_ref[s * seq + t1 * nb:s * seq + (t1 + 1) * nb, :]
                          for s in range(sps)], k1, cos_ref, sin_ref, twr_ref, twi_ref, fwd_ref, nt1, nb)
        hr = hr_ref[k1]
        hi = hi_ref[k1]
        z2 = [jnp.concatenate([yr * hr - yi * hi, yr * hi + yi * hr], axis=0).astype(BF16) for yr, yi in ys]
        dd = [jnp.dot(inv_ref[...], x, preferred_element_type=F32) for x in z2]
        twr = _lane_tile(twr_ref[k1], hr.shape[1])
        twi = _lane_tile(twi_ref[k1], hr.shape[1])
        wt = jnp.where((k1 == 0) | (k1 == n1 // 2), 1.0, 2.0) / (n1 * nb)
        for s in range(sps):
            dr, di = dd[s][:nb], dd[s][nb:]
            er = dr * twr + di * twi
            ei = di * twr - dr * twi
            for t1 in range(nt1):
                rows = slice(s * seq + t1 * nb, s * seq + (t1 + 1) * nb)
                zc_sc[rows, :] += (wt * cos_ref[k1, t1]) * er - (wt * sin_ref[k1, t1]) * ei
        return carry

    lax.fori_loop(0, nk, body, 0, unroll=3 if nk % 3 == 0 else 1)
    zz = zz_ref[...]
    o_ref[...] = x0_ref[...] * (zc_sc[...] + zz * skip_ref[...])


def _hyena(zz, x0, skip, hr, hi, consts, nb_batch, seq):
    nb, n1, nt1, nk = _hy_dims(seq)
    ct = HY_CT
    nct = HY_W // ct
    sps = max(1, min(nb_batch, HY_ROWS // seq, HY_SEQS))
    rows = sps * seq
    smem = pl.BlockSpec(memory_space=pltpu.SMEM)
    tok = pl.BlockSpec((rows, ct), lambda c, b: (b, c))
    one = dict(pipeline_mode=pl.Buffered(1)) if rows >= HY_ROWS else {}
    tok_in = pl.BlockSpec((rows, ct), lambda c, b: (b, c), **one)
    spec = pl.BlockSpec((nk, nb, ct), lambda c, b: (0, 0, c), pipeline_mode=pl.Buffered(1))
    return pl.pallas_call(
        functools.partial(_hyena_kernel, nb=nb, n1=n1, nt1=nt1, nk=nk, seq=seq, sps=sps),
        grid=(nct, nb_batch // sps),
        in_specs=[
            smem, smem, tok_in, tok_in,
            pl.BlockSpec((1, ct), lambda c, b: (0, c)),
            spec, spec,
            _resident((nk, nb, LANES)), _resident((nk, nb, LANES)),
            _resident((2 * nb, 2 * nb)), _resident((2 * nb, 2 * nb)),
        ],
        out_specs=tok,
        out_shape=jax.ShapeDtypeStruct((nb_batch * seq, HY_W), F32),
        scratch_shapes=[pltpu.VMEM((rows, ct), F32)],
        compiler_params=pltpu.CompilerParams(
            dimension_semantics=("parallel", "parallel"), vmem_limit_bytes=VMEM_LIMIT),
        name="hyena",
    )(consts["cos"], consts["sin"], zz, x0, skip, hr, hi, consts["twr"], consts["twi"],
      consts["fwd"], consts["inv"])


def _filter_kernel(cos_ref, sin_ref, z_ref, tcol_ref, w1_ref, b1_ref, fq_ref, w2_ref, b2_ref,
                   w3a_ref, w3b_ref, b3a_ref, b3b_ref, dl_ref, twr_ref, twi_ref, fwd_ref,
                   hr_ref, hi_ref, h0_sc, h1_sc, a2_sc, *, nb, n1, nk):
    @pl.when(pl.program_id(1) == 0)
    def _():
        fq = fq_ref[0]
        a1 = jnp.sin(fq * (_dot_hp(z_ref[...], w1_ref[0]) + b1_ref[0]))
        a2_sc[...] = jnp.sin(fq * (_dot_hp(a1, w2_ref[0]) + b2_ref[0]))

    a2 = a2_sc[...]
    win = jnp.exp(-tcol_ref[...] * dl_ref[...])
    h0_sc[...] = (_dot_hp(a2, w3a_ref[0]) + b3a_ref[0]) * win
    h1 = (_dot_hp(a2, w3b_ref[0]) + b3b_ref[0]) * win
    rowid = lax.broadcasted_iota(jnp.int32, h1.shape, 0)
    h1_sc[...] = jnp.where(rowid == 0, 0.0, h1)

    def body(k1, carry):
        (ar, ai), (br, bi) = _hy_forward(
            [lambda t1: h0_sc[t1 * nb:(t1 + 1) * nb, :], lambda t1: h1_sc[t1 * nb:(t1 + 1) * nb, :]],
            k1, cos_ref, sin_ref, twr_ref, twi_ref, fwd_ref, n1 // 2, nb)
        hr_ref[0, k1] = ar + br
        hi_ref[0, k1] = ai - bi
        return carry

    lax.fori_loop(0, nk, body, 0)


def _filter_feats(seq):
    t = np.linspace(0.0, 1.0, seq)[:, None]
    bands = (HY_EMB - 1) // 2
    f = np.linspace(1e-4, bands - 1, bands)[None, :]
    wpos = 2.0 * np.pi * np.arange(seq)[:, None] / seq
    z = np.concatenate([t, np.cos(f * wpos), -np.sin(f * wpos)], axis=-1)
    z = np.pad(z, ((0, 0), (0, LANES - HY_EMB)))
    tcol = np.repeat(t, LANES, axis=1)
    return jnp.asarray(z, F32), jnp.asarray(tcol, F32)


def _filter_spectra(seq, consts, fp):
    nb, n1, nt1, nk = _hy_dims(seq)
    ct = FILTER_CT
    nct = HY_W // ct
    z, tcol = _filter_feats(seq)
    smem = pl.BlockSpec(memory_space=pltpu.SMEM)
    per_layer = lambda r, c: pl.BlockSpec((1, r, c), lambda l, j: (l, 0, 0))
    out = pl.BlockSpec((1, nk, nb, ct), lambda l, j: (l, 0, 0, j))
    return pl.pallas_call(
        functools.partial(_filter_kernel, nb=nb, n1=n1, nk=nk),
        grid=(DEPTH, nct),
        in_specs=[
            smem, smem, _full((seq, LANES)), _full((seq, LANES)),
            per_layer(LANES, LANES), per_layer(1, LANES), per_layer(1, LANES),
            per_layer(LANES, LANES), per_layer(1, LANES),
            pl.BlockSpec((1, LANES, ct), lambda l, j: (l, 0, j)),
            pl.BlockSpec((1, LANES, ct), lambda l, j: (l, 0, nct + j)),
            pl.BlockSpec((1, 1, ct), lambda l, j: (l, 0, j)),
            pl.BlockSpec((1, 1, ct), lambda l, j: (l, 0, nct + j)),
            pl.BlockSpec((1, ct), lambda l, j: (0, j)),
            _full((nk, nb, LANES)), _full((nk, nb, LANES)), _full((2 * nb, 2 * nb)),
        ],
        out_specs=[out, out],
        out_shape=[jax.ShapeDtypeStruct((DEPTH, nk, nb, HY_W), F32)] * 2,
        scratch_shapes=[pltpu.VMEM((seq, ct), F32), pltpu.VMEM((seq, ct), F32),
                        pltpu.VMEM((seq, LANES), F32)],
        compiler_params=pltpu.CompilerParams(
            dimension_semantics=("arbitrary", "arbitrary"), vmem_limit_bytes=VMEM_LIMIT),
        name="hyena_filter",
    )(consts["cos"], consts["sin"], z, tcol, fp["w1"], fp["b1"], fp["fq"], fp["w2"], fp["b2"],
      fp["w3"], fp["w3"], fp["b3"], fp["b3"], fp["deltas"], consts["twr"], consts["twi"], consts["fwd"])


def _rms(x, g):
    return x * lax.rsqrt(jnp.mean(x * x, axis=-1, keepdims=True) + EPS) * g


def _mlp_kernel(x_ref, of_ref, ob_ref, gate_ref, yh_ref, mod_ref, gng_ref, n2g_ref, fg_ref,
                wout_ref, w1_ref, w2_ref, o_ref, *, final):
    mod = mod_ref[0]
    gate1 = mod[:, 2 * D_MODEL:3 * D_MODEL]
    shift2 = mod[:, 3 * D_MODEL:4 * D_MODEL]
    scale2 = mod[:, 4 * D_MODEL:5 * D_MODEL]
    gate2 = mod[:, 5 * D_MODEL:6 * D_MODEL]
    o = of_ref[0, 0] + ob_ref[0, 0]
    gate = gate_ref[...]
    mix = jnp.dot(yh_ref[...].astype(BF16), wout_ref[GDN_W:, :], preferred_element_type=F32)
    for h in range(HEADS):
        hs = slice(h * DK, (h + 1) * DK)
        oh = _rms(o[:, hs], gng_ref[...]) * _silu(gate[:, hs])
        mix = mix + jnp.dot(oh.astype(BF16), wout_ref[hs, :], preferred_element_type=F32)
    x = x_ref[...] + gate1 * mix
    h2 = (_rms(x, n2g_ref[...]) * (1.0 + scale2) + shift2).astype(BF16)
    acc = None
    for cb in range(D_FF // D_MODEL):
        cs = slice(cb * D_MODEL, (cb + 1) * D_MODEL)
        a = jnp.maximum(jnp.dot(h2, w1_ref[:, cs], preferred_element_type=F32), 0.0)
        part = jnp.dot((a * a).astype(BF16), w2_ref[cs, :], preferred_element_type=F32)
        acc = part if acc is None else acc + part
    x = x + gate2 * acc
    if final:
        x = _rms(x, fg_ref[...])
    o_ref[...] = x


def _mlp(x, o, gate, yh, mod3, row0, tiles_per_row, tiles_per_seq, tm, p, final):
    t = x.shape[0]
    tok = lambda w: pl.BlockSpec((tm, w), lambda i: (i, 0))

    def o_spec(d):
        return pl.BlockSpec((1, 1, tm, GDN_W), lambda i: (i // tiles_per_seq, d, i % tiles_per_seq, 0))

    return pl.pallas_call(
        functools.partial(_mlp_kernel, final=final),
        grid=(t // tm,),
        in_specs=[
            tok(D_MODEL), o_spec(0), o_spec(1), tok(GDN_W), tok(HY_W),
            pl.BlockSpec((1, 1, 6 * D_MODEL), lambda i: (row0 + i // tiles_per_row, 0, 0)),
            _full((1, DK)), _full((1, D_MODEL)), _full((1, D_MODEL)),
            _resident((D_MODEL, D_MODEL)), _resident((D_MODEL, D_FF)), _resident((D_FF, D_MODEL)),
        ],
        out_specs=tok(D_MODEL),
        out_shape=jax.ShapeDtypeStruct((t, D_MODEL), F32),
        compiler_params=pltpu.CompilerParams(
            dimension_semantics=("parallel",), vmem_limit_bytes=VMEM_LIMIT),
        name="mlp",
    )(x, o, o, gate, yh, mod3, p["gng"], p["n2g"], p["fg"], p["wout"], p["w1"], p["w2"])


def _pad_to(a, shape):
    return jnp.pad(a, [(0, s - d) for d, s in zip(a.shape, shape)])


def kernel(x_prompt, x_sample, state_gdn, c, c_ctx, w_ada, b_ada, norm1_g, norm2_g, w_in, gdn_conv_w, gdn_a_log, gdn_dt_bias, gdn_norm_g, hy_conv_w, hy_w1, hy_b1, hy_freq, hy_w2, hy_b2, hy_w3, hy_b3, hy_skip, w_out, w_mlp1, w_mlp2, final_g):
    nb_ctx, seq_ctx, _ = x_prompt.shape
    nb_smp, seq_smp, _ = x_sample.shape
    ctx_row = nb_smp

    cvec = _pad_to(jnp.concatenate([c, c_ctx[None]], axis=0), (MOD_ROWS, D_MODEL))
    mod = _modulation(cvec, w_ada, b_ada)

    layers = []
    for l in range(DEPTH):
        layers.append(dict(
            n1g=norm1_g[l][None], n2g=norm2_g[l][None], gng=gdn_norm_g[l][None], fg=final_g[None],
            wq=w_in[l][:, :OFF_A].astype(BF16),
            wab=_pad_to(w_in[l][:, OFF_A:OFF_HY], (D_MODEL, LANES)).astype(BF16),
            why=w_in[l][:, OFF_HY:].astype(BF16),
            cw=gdn_conv_w[l], hcw=hy_conv_w[l],
            alog=_pad_to(gdn_a_log[l].reshape(1, -1), (1, LANES)),
            dtb=_pad_to(gdn_dt_bias[l].reshape(1, -1), (1, LANES)),
            skip=hy_skip[l][None],
            wout=w_out[l].astype(BF16), w1=w_mlp1[l].astype(BF16), w2=w_mlp2[l].astype(BF16),
        ))

    max_decay = math.log(1e-2) / 0.3
    min_decay = math.log(1e-2) / 1.5
    fparams = dict(
        w1=_pad_to(hy_w1, (DEPTH, LANES, LANES)), b1=_pad_to(hy_b1[:, None], (DEPTH, 1, LANES)),
        fq=_pad_to(hy_freq[:, None], (DEPTH, 1, LANES)),
        w2=_pad_to(hy_w2, (DEPTH, LANES, LANES)), b2=_pad_to(hy_b2[:, None], (DEPTH, 1, LANES)),
        w3=_pad_to(hy_w3, (DEPTH, LANES, N_DIR * HY_W)), b3=hy_b3[:, None],
        deltas=jnp.asarray(np.abs(np.linspace(min_decay, max_decay, HY_W))[None], F32),
    )

    def run_group(x3, s0_of_layer, row0, per_batch_mod, seg):
        nb, seq, _ = x3.shape
        consts = _hy_consts(seq)
        spec_r, spec_i = _filter_spectra(seq, consts, fparams)
        x = x3.reshape(nb * seq, D_MODEL)
        tiles_per_seq = seq // TOKEN_TILE
        tiles_per_row = tiles_per_seq if per_batch_mod else nb * tiles_per_seq
        mlp_tile = min(seq, MLP_TILE)
        mlp_tps = seq // mlp_tile
        states = []
        for l in range(DEPTH):
            p = layers[l]
            mod3 = mod[l].reshape(MOD_ROWS, 1, 6 * D_MODEL)
            qkv, gate, gb, x0, zz = _proj(x, mod3, row0, tiles_per_row, seg, p)
            o, s_fin = _gdn(qkv, gb, s0_of_layer(l), nb, seq)
            yh = _hyena(zz, x0, p["skip"], spec_r[l], spec_i[l], consts, nb, seq)
            x = _mlp(x, o, gate, yh, mod3, row0, tiles_per_row * mlp_tps // tiles_per_seq, mlp_tps, mlp_tile,
                     p, l == DEPTH - 1)
            states.append(s_fin)
        return x.reshape(nb, seq, D_MODEL), states

    zero_state = jnp.zeros((nb_ctx, N_DIR, HEADS, DK, DK), F32)
    y_prompt, ctx_states = run_group(x_prompt, lambda l: zero_state, ctx_row, False, seq_ctx)
    new_state = jnp.stack(ctx_states, axis=1).astype(x_prompt.dtype)
    y_sample, _ = run_group(x_sample, lambda l: state_gdn[:, l].astype(F32), 0, True, GRID_W)
    return (y_prompt, y_sample, new_state)
```

```python
import functools
import math

import numpy as np
import jax
import jax.numpy as jnp
from jax import lax
from jax.experimental import pallas as pl
from jax.experimental.pallas import tpu as pltpu

F32 = jnp.float32
BF16 = jnp.bfloat16

D_MODEL = 1024
DEPTH = 4
GRID_W = 64
N_DIR = 2
HEADS = 4
DK = 128
GDN_W = HEADS * DK
GDN_CONV = 5
HY_W = D_MODEL - GDN_W
HY_CONV = 3
HY_EMB = 33
HY_FH = 64
D_FF = 4 * D_MODEL
EPS = 1e-6
OFF_G = 3 * GDN_W
OFF_A = 4 * GDN_W
OFF_B = OFF_A + N_DIR * HEADS
OFF_HY = OFF_B + N_DIR * HEADS
IN_COLS = OFF_HY + 3 * HY_W

LANES = 128
HALO = 8
CHUNK = 64
PACK = HEADS * CHUNK
GDN_GROUP = 8
TOKEN_TILE = 512
MLP_TILE = 512
GDN_BLOCK = 256
GDN_BATCH = 4
HY_BLOCK = 512
HY_CT = 256
FILTER_CT = 128
HY_ROWS = 4096
HY_SEQS = 8
MOD_ROWS = 8
VMEM_LIMIT = 56 * 1024 * 1024


def _split2(x):
    hi = x.astype(BF16)
    lo = (x - hi.astype(F32)).astype(BF16)
    return hi, lo


def _dot_hp(a, b):
    ah, al = _split2(a)
    bh, bl = _split2(b)
    return (jnp.dot(ah, bh, preferred_element_type=F32)
            + (jnp.dot(ah, bl, preferred_element_type=F32)
               + jnp.dot(al, bh, preferred_element_type=F32)))


def _silu(y):
    return y / (1.0 + jnp.exp(-y))


def _sigmoid(y):
    return 1.0 / (1.0 + jnp.exp(-y))


def _full(shape):
    nd = len(shape)
    return pl.BlockSpec(shape, lambda *_: (0,) * nd)


def _resident(shape):
    nd = len(shape)
    return pl.BlockSpec(shape, lambda *_: (0,) * nd, pipeline_mode=pl.Buffered(1))


def _mod_kernel(c_ref, w_ref, b_ref, o_ref):
    c = c_ref[...]
    o_ref[0] = _dot_hp(_silu(c), w_ref[0]) + b_ref[0]


def _modulation(cvec, w_ada, b_ada):
    tn = 1536
    n6 = 6 * D_MODEL
    return pl.pallas_call(
        _mod_kernel,
        grid=(DEPTH, n6 // tn),
        in_specs=[
            pl.BlockSpec((MOD_ROWS, D_MODEL), lambda l, j: (0, 0)),
            pl.BlockSpec((1, D_MODEL, tn), lambda l, j: (l, 0, j)),
            pl.BlockSpec((1, 1, tn), lambda l, j: (l, 0, j)),
        ],
        out_specs=pl.BlockSpec((1, MOD_ROWS, tn), lambda l, j: (l, 0, j)),
        out_shape=jax.ShapeDtypeStruct((DEPTH, MOD_ROWS, n6), F32),
        compiler_params=pltpu.CompilerParams(
            dimension_semantics=("arbitrary", "arbitrary"), vmem_limit_bytes=VMEM_LIMIT),
        name="modulation",
    )(cvec, w_ada, b_ada.reshape(DEPTH, 1, n6))


def _store_segments(buf, p, seg):
    st = seg + HALO
    for cb in range(p.shape[1] // LANES):
        for s in range(p.shape[0] // seg):
            buf[cb, s * st + HALO:s * st + HALO + seg, :] = p[s * seg:(s + 1) * seg, cb * LANES:(cb + 1) * LANES]


def _seg_conv(buf, bcol, w_ref, col0, width, seg, rows):
    pad = width // 2
    st = seg + HALO
    outs = []
    for s in range(rows // seg):
        acc = None
        for j in range(width):
            lo = s * st + HALO + j - pad
            term = buf[bcol // LANES, lo:lo + seg, :] * w_ref[j:j + 1, col0:col0 + LANES]
            acc = term if acc is None else acc + term
        outs.append(acc)
    return outs[0] if len(outs) == 1 else jnp.concatenate(outs, axis=0)


def _proj_kernel(x_ref, mod_ref, n1g_ref, wq_ref, wab_ref, why_ref, cw_ref, hcw_ref, alog_ref, dtb_ref,
                 qkv_ref, gate_ref, gb_ref, x0_ref, zz_ref, halo_sc, *, seg):
    tm = x_ref.shape[0]
    x = x_ref[...]
    mod = mod_ref[0]
    shift = mod[:, 0:D_MODEL]
    scale = mod[:, D_MODEL:2 * D_MODEL]
    hn = x * lax.rsqrt(jnp.mean(x * x, axis=-1, keepdims=True) + EPS) * n1g_ref[...]
    hn = (hn * (1.0 + scale) + shift).astype(BF16)
    for k in range(halo_sc.shape[0]):
        for cb in range(halo_sc.shape[1]):
            for s in range(tm // seg + 1):
                halo_sc[k, cb, s * (seg + HALO):s * (seg + HALO) + HALO, :] = jnp.zeros((HALO, LANES), F32)

    for part in range(3):
        buf = halo_sc.at[part]
        _store_segments(buf, jnp.dot(hn, wq_ref[:, part * GDN_W:(part + 1) * GDN_W],
                                     preferred_element_type=F32), seg)
        for h in range(HEADS):
            c0 = part * GDN_W + h * DK
            y = _silu(_seg_conv(buf, h * DK, cw_ref, c0, GDN_CONV, seg, tm))
            if part < 2:
                y = y * lax.rsqrt(jnp.sum(y * y, axis=-1, keepdims=True) + EPS)
            if part == 0:
                y = y * (DK ** -0.5)
            qkv_ref[:, c0:c0 + DK] = y
    gate_ref[...] = jnp.dot(hn, wq_ref[:, OFF_G:OFF_A], preferred_element_type=F32)

    z = jnp.dot(hn, wab_ref[...], preferred_element_type=F32)
    za = z + dtb_ref[...]
    softplus = jnp.maximum(za, 0.0) + jnp.log1p(jnp.exp(-jnp.abs(za)))
    g = -jnp.exp(alog_ref[...]) * softplus
    lane = lax.broadcasted_iota(jnp.int32, (tm, LANES), 1)
    gb_ref[...] = jnp.where(lane < N_DIR * HEADS, g, _sigmoid(z))

    for part in range(3):
        _store_segments(halo_sc.at[part], jnp.dot(hn, why_ref[:, part * HY_W:(part + 1) * HY_W],
                                                  preferred_element_type=F32), seg)
    for cb in range(HY_W // LANES):
        sl = slice(cb * LANES, (cb + 1) * LANES)
        x0_ref[:, sl] = _seg_conv(halo_sc.at[0], cb * LANES, hcw_ref, cb * LANES, HY_CONV, seg, tm)
        x1 = _seg_conv(halo_sc.at[1], cb * LANES, hcw_ref, HY_W + cb * LANES, HY_CONV, seg, tm)
        hv = _seg_conv(halo_sc.at[2], cb * LANES, hcw_ref, 2 * HY_W + cb * LANES, HY_CONV, seg, tm)
        zz_ref[:, sl] = x1 * hv


def _proj(x, mod3, row0, tiles_per_row, seg, tm, p):
    t = x.shape[0]
    tok = lambda w: pl.BlockSpec((tm, w), lambda i: (i, 0))
    return pl.pallas_call(
        functools.partial(_proj_kernel, seg=seg),
        grid=(t // tm,),
        in_specs=[
            tok(D_MODEL),
            pl.BlockSpec((1, 1, 6 * D_MODEL), lambda i: (row0 + i // tiles_per_row, 0, 0)),
            _full((1, D_MODEL)),
            _resident((D_MODEL, OFF_A)),
            _resident((D_MODEL, LANES)),
            _resident((D_MODEL, 3 * HY_W)),
            _full((GDN_CONV, 3 * GDN_W)),
            _full((HY_CONV, 3 * HY_W)),
            _full((1, LANES)),
            _full((1, LANES)),
        ],
        out_specs=[tok(3 * GDN_W), tok(GDN_W), tok(LANES), tok(HY_W), tok(HY_W)],
        out_shape=[
            jax.ShapeDtypeStruct((t, 3 * GDN_W), F32),
            jax.ShapeDtypeStruct((t, GDN_W), F32),
            jax.ShapeDtypeStruct((t, LANES), F32),
            jax.ShapeDtypeStruct((t, HY_W), F32),
            jax.ShapeDtypeStruct((t, HY_W), F32),
        ],
        scratch_shapes=[pltpu.VMEM((3, GDN_W // LANES, (tm // seg) * (seg + HALO) + HALO, LANES), F32)],
        compiler_params=pltpu.CompilerParams(
            dimension_semantics=("parallel",), vmem_limit_bytes=VMEM_LIMIT),
        name="proj",
    )(x, mod3, p["n1g"], p["wq"], p["wab"], p["why"], p["cw"], p["hcw"], p["alog"], p["dtb"])


def _aligned(x, m):
    return x if isinstance(x, int) else pl.multiple_of(x, m)


def _blockdiag(y, masks):
    return jnp.concatenate([jnp.where(m, y, jnp.zeros_like(y)) for m in masks], axis=0)


def _pdot(xs, ys, masks):
    rhs = [_blockdiag(y.astype(BF16), masks) for y in ys]
    return [jnp.dot(x.astype(BF16), r, preferred_element_type=F32) for x, r in zip(xs, rhs)]


def _unit_tri_inverse(a, eye, m16, m32, m64, hm):
    add = lambda ts, ps: [t + p for t, p in zip(ts, ps)]
    sub = lambda ts, ps: [t - p for t, p in zip(ts, ps)]
    n = [-(x * m16) for x in a]
    t = [eye + x for x in n]
    n2 = _pdot(n, n, hm)
    t = add(t, _pdot(t, n2, hm))
    n4 = _pdot(n2, n2, hm)
    t = add(t, _pdot(t, n4, hm))
    n8 = _pdot(n4, n4, hm)
    t = add(t, _pdot(t, n8, hm))
    t = sub(t, _pdot(t, _pdot([x * m32 for x in a], t, hm), hm))
    t = sub(t, _pdot(t, _pdot([x * m64 for x in a], t, hm), hm))
    return t


def _gdn_kernel(q_ref, k_ref, v_ref, gb_ref, s0_ref, o_ref, sfin_ref,
                s_sc, gwh_sc, gwl_sc, bw_sc, u_sc, w_sc, qd_sc, kd_sc, qk_sc, eg_sc,
                *, nch, group):
    d = pl.program_id(1)
    j = pl.program_id(2)
    c = CHUNK
    fwd = d == 0
    nbb = q_ref.shape[0]

    @pl.when(j == 0)
    def _():
        s_sc[...] = s0_ref[:, 0]

    row = lax.broadcasted_iota(jnp.int32, (c, LANES), 0)
    lane = lax.broadcasted_iota(jnp.int32, (c, LANES), 1)
    col = lane & (c - 1)
    r = jnp.where(fwd, row, col)
    cc = jnp.where(fwd, col, row)
    tri = r >= cc
    strict = r > cc
    strict_f = strict.astype(F32)
    eye = (row == col).astype(F32)
    m16 = ((row >> 4) == (col >> 4)).astype(F32)
    same32 = ((row >> 5) == (col >> 5)).astype(F32)
    m32 = same32 - m16
    m64 = 1.0 - same32
    first = lane < c
    pm = [first, lane >= c]
    lane_w = lax.broadcasted_iota(jnp.int32, (c, GDN_W), 1)
    hw = [(lane_w >> 7) == h for h in range(HEADS)]
    pw = [((lane_w >> 7) & 1) == q for q in range(2)]
    lane_v = lax.broadcasted_iota(jnp.int32, (c, 2 * LANES), 1)
    pv = [(lane_v >> 7) == q for q in range(2)]
    r2 = lax.broadcasted_iota(jnp.int32, (c, c), 0)
    c2 = lax.broadcasted_iota(jnp.int32, (c, c), 1)
    tri_b = (jnp.where(fwd, r2, c2) >= jnp.where(fwd, c2, r2)).astype(F32).astype(BF16)
    cum_lhs = jnp.concatenate([tri_b, jnp.ones((c, c), BF16)], axis=0)
    zero_b = jnp.zeros((c, LANES), BF16)

    def pairs(x):
        return [jnp.where(first, x[:, 2 * p * LANES:(2 * p + 1) * LANES], x[:, (2 * p + 1) * LANES:(2 * p + 2) * LANES])
                for p in range(HEADS // 2)]

    def selector(first_row):
        sel_r = lax.broadcasted_iota(jnp.int32, (LANES, GDN_W), 0)
        sel_h = lax.broadcasted_iota(jnp.int32, (LANES, GDN_W), 1) >> 7
        return (sel_r == first_row + sel_h).astype(F32).astype(BF16)

    sel_g = selector(d * HEADS)
    sel_b = selector((N_DIR + d) * HEADS)
    spread = lambda x, sel: jnp.dot(x, sel, preferred_element_type=F32)
    for bb in range(nbb):
        gb_hi, gb_lo = _split2(gb_ref[bb])
        gwh_sc[bb] = spread(gb_hi, sel_g).astype(BF16)
        gwl_sc[bb] = spread(gb_lo, sel_g).astype(BF16)
        bw_sc[bb] = spread(gb_hi, sel_b) + spread(gb_lo, sel_b)

    def cumsum(lhs, hi, lo):
        return jnp.dot(lhs, hi, preferred_element_type=F32) + jnp.dot(lhs, lo, preferred_element_type=F32)

    def precompute(it, carry):
        grp = range(group)
        if nch >= group:
            seq_of = [it // (nch // group)] * group
            chunk_of = [(it - seq_of[0] * (nch // group)) * group + i for i in grp]
        else:
            seq_of = [it * (group // nch) + i // nch for i in grp]
            chunk_of = [i % nch for i in grp]
        at = [(seq_of[i], pl.ds(_aligned(chunk_of[i] * c, c), c)) for i in grp]
        items = [(i, p) for i in grp for p in range(HEADS // 2)]
        kc = [k_ref[bb, r_, :] for bb, r_ in at]
        qc = [q_ref[bb, r_, :] for bb, r_ in at]
        gh = [gwh_sc[bb, r_, :] for bb, r_ in at]
        gl = [gwl_sc[bb, r_, :] for bb, r_ in at]
        bw = [bw_sc[bb, r_, :] for bb, r_ in at]
        gph = [pairs(x) for x in gh]
        gpl = [pairs(x) for x in gl]
        bp = [pairs(x) for x in bw]
        e = [cumsum(tri_b, jnp.where(strict, gph[i][p], zero_b), jnp.where(strict, gpl[i][p], zero_b))
             for i, p in items]
        kb = [x.astype(BF16) for x in kc]
        kq = [lax.dot_general(jnp.concatenate([kb[i], qc[i].astype(BF16)], axis=0), _blockdiag(kb[i], hw),
                              (((1,), (1,)), ((), ())), preferred_element_type=F32) for i in grp]
        decay = [jnp.where(tri, jnp.exp(x), 0.0) for x in e]
        a = [bp[i][p] * kq[i][:c, p * LANES:(p + 1) * LANES] * decay[k] * strict_f
             for k, (i, p) in enumerate(items)]
        t = _unit_tri_inverse(a, eye, m16, m32, m64, pm)
        cums =[cumsum(cum_lhs, gh[i], gl[i]) for i in grp]
        gc = [x[:c] for x in cums]
        gt = [x[c:] for x in cums]
        egc = [jnp.exp(x) for x in gc]
        vb = [v_ref[at[i][0], at[i][1], :] * bw[i] for i in grp]
        kbg = [kc[i] * bw[i] * egc[i] for i in grp]
        uw = _pdot(t, [jnp.concatenate([vb[i][:, 2 * p * LANES:(2 * p + 2) * LANES],
                                        kbg[i][:, 2 * p * LANES:(2 * p + 2) * LANES]], axis=1)
                       for i, p in items], pw)
        for k, (i, p) in enumerate(items):
            bb, r_ = at[i]
            u_sc[bb, r_, 2 * p * LANES:(2 * p + 2) * LANES] = uw[k][:, :2 * LANES]
            w_sc[bb, r_, 2 * p * LANES:(2 * p + 2) * LANES] = uw[k][:, 2 * LANES:].astype(BF16)
            qk_sc[bb, r_, p * LANES:(p + 1) * LANES] = (kq[i][c:, p * LANES:(p + 1) * LANES] * decay[k]).astype(BF16)
        for i in grp:
            bb, r_ = at[i]
            qd_sc[bb, r_, :] = (qc[i] * egc[i]).astype(BF16)
            kd_sc[bb, r_, :] = (kc[i] * jnp.exp(gt[i] - gc[i])).astype(BF16)
            eg_sc[bb, pl.ds(_aligned(chunk_of[i] * 8, 8), 8), :] = jnp.exp(gt[i][0:8, :])
        return carry

    lax.fori_loop(0, nbb * nch // group, precompute, 0)

    def recur(n, carry):
        ne = jnp.where(fwd, n, nch - 1 - n)
        rows = pl.ds(pl.multiple_of(ne * c, c), c)
        seqs = range(nbb)
        heads = [slice(h * DK, (h + 1) * DK) for h in range(HEADS)]
        s_old = [[s_sc[bb, h] for h in range(HEADS)] for bb in seqs]
        both = [[jnp.dot(jnp.concatenate([w_sc[bb, rows, hs], qd_sc[bb, rows, hs]], axis=0),
                         s_old[bb][h].astype(BF16), preferred_element_type=F32)
                 for h, hs in enumerate(heads)] for bb in seqs]
        vb = [jnp.concatenate([u_sc[bb, rows, hs] - both[bb][h][:c] for h, hs in enumerate(heads)],
                              axis=1).astype(BF16) for bb in seqs]
        upd = [[lax.dot_general(kd_sc[bb, rows, hs], vb[bb][:, hs], (((0,), (0,)), ((), ())),
                                preferred_element_type=F32) for hs in heads] for bb in seqs]
        for bb in seqs:
            eg = eg_sc[bb, pl.ds(pl.multiple_of(ne * 8, 8), 8), :][0:1, :]
            for h, hs in enumerate(heads):
                s_sc[bb, h] = s_old[bb][h] * eg[:, hs] + upd[bb][h]
        for bb in seqs:
            qkv = [jnp.dot(qk_sc[bb, rows, p * LANES:(p + 1) * LANES],
                           _blockdiag(vb[bb][:, 2 * p * LANES:(2 * p + 2) * LANES], pv),
                           preferred_element_type=F32) for p in range(HEADS // 2)]
            o_ref[bb, 0, rows, :] = (jnp.concatenate([both[bb][h][c:] for h in range(HEADS)], axis=1)
                                     + jnp.concatenate(qkv, axis=1))
        return carry

    lax.fori_loop(0, nch, recur, 0)

    @pl.when(j == pl.num_programs(2) - 1)
    def _():
        sfin_ref[:, 0] = s_sc[...]


def _gdn(qkv, gb, s0, nb, seq):
    lb = min(seq, GDN_BLOCK)
    nblk = seq // lb
    nch = lb // CHUNK

    nbb = GDN_BATCH

    def blk(d, j):
        return j + d * (nblk - 1 - 2 * j)

    def qkv_spec(part):
        return pl.BlockSpec((nbb, lb, GDN_W), lambda b, d, j: (b, blk(d, j), part))

    state_spec = pl.BlockSpec((nbb, 1, HEADS, DK, DK), lambda b, d, j: (b, d, 0, 0, 0))
    qkv3 = qkv.reshape(nb, seq, 3 * GDN_W)
    return pl.pallas_call(
        functools.partial(_gdn_kernel, nch=nch, group=min(nbb * nch, GDN_GROUP)),
        grid=(nb // nbb, N_DIR, nblk),
        in_specs=[
            qkv_spec(0), qkv_spec(1), qkv_spec(2),
            pl.BlockSpec((nbb, lb, LANES), lambda b, d, j: (b, blk(d, j), 0)),
            state_spec,
        ],
        out_specs=[
            pl.BlockSpec((nbb, 1, lb, GDN_W), lambda b, d, j: (b, d, blk(d, j), 0)),
            state_spec,
        ],
        out_shape=[
            jax.ShapeDtypeStruct((nb, N_DIR, seq, GDN_W), F32),
            jax.ShapeDtypeStruct((nb, N_DIR, HEADS, DK, DK), F32),
        ],
        scratch_shapes=[
            pltpu.VMEM((nbb, HEADS, DK, DK), F32),
            pltpu.VMEM((nbb, lb, GDN_W), BF16),
            pltpu.VMEM((nbb, lb, GDN_W), BF16),
            pltpu.VMEM((nbb, lb, GDN_W), F32),
            pltpu.VMEM((nbb, lb, GDN_W), F32),
            pltpu.VMEM((nbb, lb, GDN_W), BF16),
            pltpu.VMEM((nbb, lb, GDN_W), BF16),
            pltpu.VMEM((nbb, lb, GDN_W), BF16),
            pltpu.VMEM((nbb, lb, PACK), BF16),
            pltpu.VMEM((nbb, nch * 8, GDN_W), F32),
        ],
        compiler_params=pltpu.CompilerParams(
            dimension_semantics=("parallel", "arbitrary", "arbitrary"), vmem_limit_bytes=VMEM_LIMIT),
        name="gdn",
    )(qkv3, qkv3, qkv3, gb.reshape(nb, seq, LANES), s0)


def _hy_dims(seq):
    nb = min(HY_BLOCK, seq)
    n1 = 2 * seq // nb
    return nb, n1, n1 // 2, n1 // 2 + 1


def _hy_consts(seq):
    nb, n1, nt1, nk = _hy_dims(seq)
    n = 2 * seq
    k2 = np.arange(nb)[:, None]
    t2 = np.arange(nb)[None, :]
    ang = 2.0 * np.pi * ((k2 * t2) % nb) / nb
    fr, fi = np.cos(ang), -np.sin(ang)
    fwd = np.block([[fr, -fi], [fi, fr]])
    inv = np.block([[fr, fi], [-fi, fr]])
    k1 = np.arange(nk)[:, None]
    ang = 2.0 * np.pi * (k1 * np.arange(nb)[None, :]) / n
    twr = np.repeat(np.cos(ang)[:, :, None], LANES, axis=2)
    twi = np.repeat(-np.sin(ang)[:, :, None], LANES, axis=2)
    ang = 2.0 * np.pi * ((k1 * np.arange(n1)[None, :]) % n1) / n1
    return dict(
        fwd=jnp.asarray(fwd, F32).astype(BF16), inv=jnp.asarray(inv, F32).astype(BF16),
        twr=jnp.asarray(twr, F32), twi=jnp.asarray(twi, F32),
        cos=jnp.asarray(np.cos(ang), F32), sin=jnp.asarray(np.sin(ang), F32))


def _lane_tile(x, width):
    return x if width == LANES else jnp.concatenate([x] * (width // LANES), axis=1)


def _hy_forward(blocks, k1, cos_ref, sin_ref, twr_ref, twi_ref, fwd_ref, nt1, nb):
    cs = []
    for block in blocks:
        cr = None
        ci = None
        for t1 in range(nt1):
            xb = block(t1)
            tr = cos_ref[k1, t1] * xb
            ti = sin_ref[k1, t1] * xb
            cr = tr if cr is None else cr + tr
            ci = ti if ci is None else ci + ti
        cs.append((cr, -ci))
    width = cs[0][0].shape[1]
    twr = _lane_tile(twr_ref[k1], width)
    twi = _lane_tile(twi_ref[k1], width)
    c2 = [jnp.concatenate([cr * twr - ci * twi, cr * twi + ci * twr], axis=0).astype(BF16) for cr, ci in cs]
    ys = [jnp.dot(fwd_ref[...], x, preferred_element_type=F32) for x in c2]
    return [(y[:nb], y[nb:]) for y in ys]


def _hyena_kernel(cos_ref, sin_ref, zz_ref, x0_ref, skip_ref, hr_ref, hi_ref, twr_ref, twi_ref,
                  fwd_ref, inv_ref, o_ref, zc_sc, *, nb, n1, nt1, nk, seq, sps):
    zc_sc[...] = jnp.zeros_like(zc_sc)

    def body(k1, carry):
        ys = _hy_forward([lambda t1, s=s: zz_ref[s * seq + t1 * nb:s * seq + (t1 + 1) * nb, :]
                          for s in range(sps)], k1, cos_ref, sin_ref, twr_ref, twi_ref, fwd_ref, nt1, nb)
        hr = hr_ref[k1]
        hi = hi_ref[k1]
        z2 = [jnp.concatenate([yr * hr - yi * hi, yr * hi + yi * hr], axis=0).astype(BF16) for yr, yi in ys]
        dd = [jnp.dot(inv_ref[...], x, preferred_element_type=F32) for x in z2]
        twr = _lane_tile(twr_ref[k1], hr.shape[1])
        twi = _lane_tile(twi_ref[k1], hr.shape[1])
        wt = jnp.where((k1 == 0) | (k1 == n1 // 2), 1.0, 2.0) / (n1 * nb)
        for s in range(sps):
            dr, di = dd[s][:nb], dd[s][nb:]
            er = dr * twr + di * twi
            ei = di * twr - dr * twi
            for t1 in range(nt1):
                rows = slice(s * seq + t1 * nb, s * seq + (t1 + 1) * nb)
                zc_sc[rows, :] += (wt * cos_ref[k1, t1]) * er - (wt * sin_ref[k1, t1]) * ei
        return carry

    lax.fori_loop(0, nk, body, 0, unroll=3 if nk % 3 == 0 else 1)
    zz = zz_ref[...]
    o_ref[...] = x0_ref[...] * (zc_sc[...] + zz * skip_ref[...])


def _hyena(zz, x0, skip, hr, hi, consts, nb_batch, seq):
    nb, n1, nt1, nk = _hy_dims(seq)
    ct = HY_CT
    nct = HY_W // ct
    sps = max(1, min(nb_batch, HY_ROWS // seq, HY_SEQS))
    rows = sps * seq
    smem = pl.BlockSpec(memory_space=pltpu.SMEM)
    tok = pl.BlockSpec((rows, ct), lambda c, b: (b, c))
    one = dict(pipeline_mode=pl.Buffered(1)) if rows >= HY_ROWS else {}
    tok_in = pl.BlockSpec((rows, ct), lambda c, b: (b, c), **one)
    spec = pl.BlockSpec((nk, nb, ct), lambda c, b: (0, 0, c), pipeline_mode=pl.Buffered(1))
    return pl.pallas_call(
        functools.partial(_hyena_kernel, nb=nb, n1=n1, nt1=nt1, nk=nk, seq=seq, sps=sps),
        grid=(nct, nb_batch // sps),
        in_specs=[
            smem, smem, tok_in, tok_in,
            pl.BlockSpec((1, ct), lambda c, b: (0, c)),
            spec, spec,
            _resident((nk, nb, LANES)), _resident((nk, nb, LANES)),
            _resident((2 * nb, 2 * nb)), _resident((2 * nb, 2 * nb)),
        ],
        out_specs=tok,
        out_shape=jax.ShapeDtypeStruct((nb_batch * seq, HY_W), F32),
        scratch_shapes=[pltpu.VMEM((rows, ct), F32)],
        compiler_params=pltpu.CompilerParams(
            dimension_semantics=("parallel", "parallel"), vmem_limit_bytes=VMEM_LIMIT),
        name="hyena",
    )(consts["cos"], consts["sin"], zz, x0, skip, hr, hi, consts["twr"], consts["twi"],
      consts["fwd"], consts["inv"])


def _filter_kernel(cos_ref, sin_ref, z_ref, tcol_ref, w1_ref, b1_ref, fq_ref, w2_ref, b2_ref,
                   w3a_ref, w3b_ref, b3a_ref, b3b_ref, dl_ref, twr_ref, twi_ref, fwd_ref,
                   hr_ref, hi_ref, hh_sc, a2_sc, *, nb, n1, nk):
    ct = hr_ref.shape[-1]

    @pl.when(pl.program_id(1) == 0)
    def _():
        fq = fq_ref[0]
        a1 = jnp.sin(fq * (_dot_hp(z_ref[...], w1_ref[0]) + b1_ref[0]))
        a2_sc[...] = jnp.sin(fq * (_dot_hp(a1, w2_ref[0]) + b2_ref[0]))

    a2 = a2_sc[...]
    win = jnp.exp(-tcol_ref[...] * dl_ref[...])
    hh_sc[:, :ct] = (_dot_hp(a2, w3a_ref[0]) + b3a_ref[0]) * win
    h1 = (_dot_hp(a2, w3b_ref[0]) + b3b_ref[0]) * win
    rowid = lax.broadcasted_iota(jnp.int32, h1.shape, 0)
    hh_sc[:, ct:] = jnp.where(rowid == 0, 0.0, h1)

    def body(k1, carry):
        ((yr, yi),) = _hy_forward([lambda t1: hh_sc[t1 * nb:(t1 + 1) * nb, :]],
                                  k1, cos_ref, sin_ref, twr_ref, twi_ref, fwd_ref, n1 // 2, nb)
        hr_ref[0, k1] = yr[:, :ct] + yr[:, ct:]
        hi_ref[0, k1] = yi[:, :ct] - yi[:, ct:]
        return carry

    lax.fori_loop(0, nk, body, 0)


def _filter_feats(seq):
    t = np.linspace(0.0, 1.0, seq)[:, None]
    bands = (HY_EMB - 1) // 2
    f = np.linspace(1e-4, bands - 1, bands)[None, :]
    wpos = 2.0 * np.pi * np.arange(seq)[:, None] / seq
    z = np.concatenate([t, np.cos(f * wpos), -np.sin(f * wpos)], axis=-1)
    z = np.pad(z, ((0, 0), (0, LANES - HY_EMB)))
    tcol = np.repeat(t, LANES, axis=1)
    return jnp.asarray(z, F32), jnp.asarray(tcol, F32)


def _filter_spectra(seq, consts, fp):
    nb, n1, nt1, nk = _hy_dims(seq)
    ct = FILTER_CT
    nct = HY_W // ct
    z, tcol = _filter_feats(seq)
    smem = pl.BlockSpec(memory_space=pltpu.SMEM)
    per_layer = lambda r, c: pl.BlockSpec((1, r, c), lambda l, j: (l, 0, 0))
    out = pl.BlockSpec((1, nk, nb, ct), lambda l, j: (l, 0, 0, j))
    return pl.pallas_call(
        functools.partial(_filter_kernel, nb=nb, n1=n1, nk=nk),
        grid=(DEPTH, nct),
        in_specs=[
            smem, smem, _full((seq, LANES)), _full((seq, LANES)),
            per_layer(LANES, LANES), per_layer(1, LANES), per_layer(1, LANES),
            per_layer(LANES, LANES), per_layer(1, LANES),
            pl.BlockSpec((1, LANES, ct), lambda l, j: (l, 0, j)),
            pl.BlockSpec((1, LANES, ct), lambda l, j: (l, 0, nct + j)),
            pl.BlockSpec((1, 1, ct), lambda l, j: (l, 0, j)),
            pl.BlockSpec((1, 1, ct), lambda l, j: (l, 0, nct + j)),
            pl.BlockSpec((1, ct), lambda l, j: (0, j)),
            _full((nk, nb, LANES)), _full((nk, nb, LANES)), _full((2 * nb, 2 * nb)),
        ],
        out_specs=[out, out],
        out_shape=[jax.ShapeDtypeStruct((DEPTH, nk, nb, HY_W), F32)] * 2,
        scratch_shapes=[pltpu.VMEM((seq, 2 * ct), F32), pltpu.VMEM((seq, LANES), F32)],
        compiler_params=pltpu.CompilerParams(
            dimension_semantics=("arbitrary", "arbitrary"), vmem_limit_bytes=VMEM_LIMIT),
        name="hyena_filter",
    )(consts["cos"], consts["sin"], z, tcol, fp["w1"], fp["b1"], fp["fq"], fp["w2"], fp["b2"],
      fp["w3"], fp["w3"], fp["b3"], fp["b3"], fp["deltas"], consts["twr"], consts["twi"], consts["fwd"])


def _rms(x, g):
    return x * lax.rsqrt(jnp.mean(x * x, axis=-1, keepdims=True) + EPS) * g


def _mlp_kernel(x_ref, of_ref, ob_ref, gate_ref, yh_ref, mod_ref, gng_ref, n2g_ref, fg_ref,
                wout_ref, w1_ref, w2_ref, o_ref, *, final):
    mod = mod_ref[0]
    gate1 = mod[:, 2 * D_MODEL:3 * D_MODEL]
    shift2 = mod[:, 3 * D_MODEL:4 * D_MODEL]
    scale2 = mod[:, 4 * D_MODEL:5 * D_MODEL]
    gate2 = mod[:, 5 * D_MODEL:6 * D_MODEL]
    o = of_ref[0, 0] + ob_ref[0, 0]
    gate = gate_ref[...]
    mix = jnp.dot(yh_ref[...].astype(BF16), wout_ref[GDN_W:, :], preferred_element_type=F32)
    for h in range(HEADS):
        hs = slice(h * DK, (h + 1) * DK)
        oh = _rms(o[:, hs], gng_ref[...]) * _silu(gate[:, hs])
        mix = mix + jnp.dot(oh.astype(BF16), wout_ref[hs, :], preferred_element_type=F32)
    x = x_ref[...] + gate1 * mix
    h2 = (_rms(x, n2g_ref[...]) * (1.0 + scale2) + shift2).astype(BF16)
    acc = None
    for cb in range(D_FF // D_MODEL):
        cs = slice(cb * D_MODEL, (cb + 1) * D_MODEL)
        a = jnp.maximum(jnp.dot(h2, w1_ref[:, cs], preferred_element_type=F32), 0.0)
        part = jnp.dot((a * a).astype(BF16), w2_ref[cs, :], preferred_element_type=F32)
        acc = part if acc is None else acc + part
    x = x + gate2 * acc
    if final:
        x = _rms(x, fg_ref[...])
    o_ref[...] = x


def _mlp(x, o, gate, yh, mod3, row0, tiles_per_row, tiles_per_seq, tm, p, final):
    t = x.shape[0]
    tok = lambda w: pl.BlockSpec((tm, w), lambda i: (i, 0))

    def o_spec(d):
        return pl.BlockSpec((1, 1, tm, GDN_W), lambda i: (i // tiles_per_seq, d, i % tiles_per_seq, 0))

    return pl.pallas_call(
        functools.partial(_mlp_kernel, final=final),
        grid=(t // tm,),
        in_specs=[
            tok(D_MODEL), o_spec(0), o_spec(1), tok(GDN_W), tok(HY_W),
            pl.BlockSpec((1, 1, 6 * D_MODEL), lambda i: (row0 + i // tiles_per_row, 0, 0)),
            _full((1, DK)), _full((1, D_MODEL)), _full((1, D_MODEL)),
            _resident((D_MODEL, D_MODEL)), _resident((D_MODEL, D_FF)), _resident((D_FF, D_MODEL)),
        ],
        out_specs=tok(D_MODEL),
        out_shape=jax.ShapeDtypeStruct((t, D_MODEL), F32),
        compiler_params=pltpu.CompilerParams(
            dimension_semantics=("parallel",), vmem_limit_bytes=VMEM_LIMIT),
        name="mlp",
    )(x, o, o, gate, yh, mod3, p["gng"], p["n2g"], p["fg"], p["wout"], p["w1"], p["w2"])


def _pad_to(a, shape):
    return jnp.pad(a, [(0, s - d) for d, s in zip(a.shape, shape)])


def kernel(x_prompt, x_sample, state_gdn, c, c_ctx, w_ada, b_ada, norm1_g, norm2_g, w_in, gdn_conv_w, gdn_a_log, gdn_dt_bias, gdn_norm_g, hy_conv_w, hy_w1, hy_b1, hy_freq, hy_w2, hy_b2, hy_w3, hy_b3, hy_skip, w_out, w_mlp1, w_mlp2, final_g):
    nb_ctx, seq_ctx, _ = x_prompt.shape
    nb_smp, seq_smp, _ = x_sample.shape
    ctx_row = nb_smp

    cvec = _pad_to(jnp.concatenate([c, c_ctx[None]], axis=0), (MOD_ROWS, D_MODEL))
    mod = _modulation(cvec, w_ada, b_ada)

    layers = []
    for l in range(DEPTH):
        layers.append(dict(
            n1g=norm1_g[l][None], n2g=norm2_g[l][None], gng=gdn_norm_g[l][None], fg=final_g[None],
            wq=w_in[l][:, :OFF_A].astype(BF16),
            wab=_pad_to(w_in[l][:, OFF_A:OFF_HY], (D_MODEL, LANES)).astype(BF16),
            why=w_in[l][:, OFF_HY:].astype(BF16),
            cw=gdn_conv_w[l], hcw=hy_conv_w[l],
            alog=_pad_to(gdn_a_log[l].reshape(1, -1), (1, LANES)),
            dtb=_pad_to(gdn_dt_bias[l].reshape(1, -1), (1, LANES)),
            skip=hy_skip[l][None],
            wout=w_out[l].astype(BF16), w1=w_mlp1[l].astype(BF16), w2=w_mlp2[l].astype(BF16),
        ))

    max_decay = math.log(1e-2) / 0.3
    min_decay = math.log(1e-2) / 1.5
    fparams = dict(
        w1=_pad_to(hy_w1, (DEPTH, LANES, LANES)), b1=_pad_to(hy_b1[:, None], (DEPTH, 1, LANES)),
        fq=_pad_to(hy_freq[:, None], (DEPTH, 1, LANES)),
        w2=_pad_to(hy_w2, (DEPTH, LANES, LANES)), b2=_pad_to(hy_b2[:, None], (DEPTH, 1, LANES)),
        w3=_pad_to(hy_w3, (DEPTH, LANES, N_DIR * HY_W)), b3=hy_b3[:, None],
        deltas=jnp.asarray(np.abs(np.linspace(min_decay, max_decay, HY_W))[None], F32),
    )

    def run_group(x3, s0_of_layer, row0, per_batch_mod, seg):
        nb, seq, _ = x3.shape
        consts = _hy_consts(seq)
        spec_r, spec_i = _filter_spectra(seq, consts, fparams)
        x = x3.reshape(nb * seq, D_MODEL)
        proj_tile = min(seq, TOKEN_TILE)
        tiles_per_seq = seq // proj_tile
        tiles_per_row = tiles_per_seq if per_batch_mod else nb * tiles_per_seq
        mlp_tile = min(seq, MLP_TILE)
        mlp_tps = seq // mlp_tile
        states = []
        for l in range(DEPTH):
            p = layers[l]
            mod3 = mod[l].reshape(MOD_ROWS, 1, 6 * D_MODEL)
            qkv, gate, gb, x0, zz = _proj(x, mod3, row0, tiles_per_row, seg, proj_tile, p)
            o, s_fin = _gdn(qkv, gb, s0_of_layer(l), nb, seq)
            yh = _hyena(zz, x0, p["skip"], spec_r[l], spec_i[l], consts, nb, seq)
            x = _mlp(x, o, gate, yh, mod3, row0, tiles_per_row * mlp_tps // tiles_per_seq, mlp_tps, mlp_tile,
                     p, l == DEPTH - 1)
            states.append(s_fin)
        return x.reshape(nb, seq, D_MODEL), states

    zero_state = jnp.zeros((nb_ctx, N_DIR, HEADS, DK, DK), F32)
    y_prompt, ctx_states = run_group(x_prompt, lambda l: zero_state, ctx_row, False, seq_ctx)
    new_state = jnp.stack(ctx_states, axis=1).astype(x_prompt.dtype)
    y_sample, _ = run_group(x_sample, lambda l: state_gdn[:, l].astype(F32), 0, True, GRID_W)
    return (y_prompt, y_sample, new_state)
```

```python
import functools
import math

import numpy as np
import jax
import jax.numpy as jnp
from jax import lax
from jax.experimental import pallas as pl
from jax.experimental.pallas import tpu as pltpu

F32 = jnp.float32
BF16 = jnp.bfloat16

D_MODEL = 1024
DEPTH = 4
GRID_W = 64
N_DIR = 2
HEADS = 4
DK = 128
GDN_W = HEADS * DK
GDN_CONV = 5
HY_W = D_MODEL - GDN_W
HY_CONV = 3
HY_EMB = 33
HY_FH = 64
D_FF = 4 * D_MODEL
EPS = 1e-6
OFF_G = 3 * GDN_W
OFF_A = 4 * GDN_W
OFF_B = OFF_A + N_DIR * HEADS
OFF_HY = OFF_B + N_DIR * HEADS
IN_COLS = OFF_HY + 3 * HY_W

LANES = 128
HALO = 8
CHUNK = 64
PACK = HEADS * CHUNK
GDN_GROUP = 8
TOKEN_TILE = 512
MLP_TILE = 512
GDN_BLOCK = 256
GDN_BATCH = 4
HY_BLOCK = 512
HY_CT = 256
FILTER_CT = 128
HY_ROWS = 4096
HY_SEQS = 8
MOD_ROWS = 8
VMEM_LIMIT = 56 * 1024 * 1024


def _split2(x):
    hi = x.astype(BF16)
    lo = (x - hi.astype(F32)).astype(BF16)
    return hi, lo


def _dot_hp(a, b):
    ah, al = _split2(a)
    bh, bl = _split2(b)
    return (jnp.dot(ah, bh, preferred_element_type=F32)
            + (jnp.dot(ah, bl, preferred_element_type=F32)
               + jnp.dot(al, bh, preferred_element_type=F32)))


def _silu(y):
    return y / (1.0 + jnp.exp(-y))


def _sigmoid(y):
    return 1.0 / (1.0 + jnp.exp(-y))


def _full(shape):
    nd = len(shape)
    return pl.BlockSpec(shape, lambda *_: (0,) * nd)


def _resident(shape):
    nd = len(shape)
    return pl.BlockSpec(shape, lambda *_: (0,) * nd, pipeline_mode=pl.Buffered(1))


def _mod_kernel(c_ref, w_ref, b_ref, o_ref):
    c = c_ref[...]
    o_ref[0] = _dot_hp(_silu(c), w_ref[0]) + b_ref[0]


def _modulation(cvec, w_ada, b_ada):
    tn = 1536
    n6 = 6 * D_MODEL
    return pl.pallas_call(
        _mod_kernel,
        grid=(DEPTH, n6 // tn),
        in_specs=[
            pl.BlockSpec((MOD_ROWS, D_MODEL), lambda l, j: (0, 0)),
            pl.BlockSpec((1, D_MODEL, tn), lambda l, j: (l, 0, j)),
            pl.BlockSpec((1, 1, tn), lambda l, j: (l, 0, j)),
        ],
        out_specs=pl.BlockSpec((1, MOD_ROWS, tn), lambda l, j: (l, 0, j)),
        out_shape=jax.ShapeDtypeStruct((DEPTH, MOD_ROWS, n6), F32),
        compiler_params=pltpu.CompilerParams(
            dimension_semantics=("arbitrary", "arbitrary"), vmem_limit_bytes=VMEM_LIMIT),
        name="modulation",
    )(cvec, w_ada, b_ada.reshape(DEPTH, 1, n6))


def _store_segments(buf, p, seg):
    st = seg + HALO
    for cb in range(p.shape[1] // LANES):
        for s in range(p.shape[0] // seg):
            buf[cb, s * st + HALO:s * st + HALO + seg, :] = p[s * seg:(s + 1) * seg, cb * LANES:(cb + 1) * LANES]


def _seg_conv(buf, bcol, w_ref, col0, width, seg, rows):
    pad = width // 2
    st = seg + HALO
    outs = []
    for s in range(rows // seg):
        acc = None
        for j in range(width):
            lo = s * st + HALO + j - pad
            term = buf[bcol // LANES, lo:lo + seg, :] * w_ref[j:j + 1, col0:col0 + LANES]
            acc = term if acc is None else acc + term
        outs.append(acc)
    return outs[0] if len(outs) == 1 else jnp.concatenate(outs, axis=0)


def _proj_kernel(x_ref, mod_ref, n1g_ref, wq_ref, wab_ref, why_ref, cw_ref, hcw_ref, alog_ref, dtb_ref,
                 qkv_ref, gate_ref, gb_ref, x0_ref, zz_ref, halo_sc, *, seg):
    tm = x_ref.shape[0]
    x = x_ref[...]
    mod = mod_ref[0]
    shift = mod[:, 0:D_MODEL]
    scale = mod[:, D_MODEL:2 * D_MODEL]
    hn = x * lax.rsqrt(jnp.mean(x * x, axis=-1, keepdims=True) + EPS) * n1g_ref[...]
    hn = (hn * (1.0 + scale) + shift).astype(BF16)
    for k in range(halo_sc.shape[0]):
        for cb in range(halo_sc.shape[1]):
            for s in range(tm // seg + 1):
                halo_sc[k, cb, s * (seg + HALO):s * (seg + HALO) + HALO, :] = jnp.zeros((HALO, LANES), F32)

    for part in range(3):
        buf = halo_sc.at[part]
        _store_segments(buf, jnp.dot(hn, wq_ref[:, part * GDN_W:(part + 1) * GDN_W],
                                     preferred_element_type=F32), seg)
        for h in range(HEADS):
            c0 = part * GDN_W + h * DK
            y = _silu(_seg_conv(buf, h * DK, cw_ref, c0, GDN_CONV, seg, tm))
            if part < 2:
                y = y * lax.rsqrt(jnp.sum(y * y, axis=-1, keepdims=True) + EPS)
            if part == 0:
                y = y * (DK ** -0.5)
            qkv_ref[:, c0:c0 + DK] = y
    gate_ref[...] = jnp.dot(hn, wq_ref[:, OFF_G:OFF_A], preferred_element_type=F32)

    z = jnp.dot(hn, wab_ref[...], preferred_element_type=F32)
    za = z + dtb_ref[...]
    softplus = jnp.maximum(za, 0.0) + jnp.log1p(jnp.exp(-jnp.abs(za)))
    g = -jnp.exp(alog_ref[...]) * softplus
    lane = lax.broadcasted_iota(jnp.int32, (tm, LANES), 1)
    gb_ref[...] = jnp.where(lane < N_DIR * HEADS, g, _sigmoid(z))

    for part in range(3):
        _store_segments(halo_sc.at[part], jnp.dot(hn, why_ref[:, part * HY_W:(part + 1) * HY_W],
                                                  preferred_element_type=F32), seg)
    for cb in range(HY_W // LANES):
        sl = slice(cb * LANES, (cb + 1) * LANES)
        x0_ref[:, sl] = _seg_conv(halo_sc.at[0], cb * LANES, hcw_ref, cb * LANES, HY_CONV, seg, tm)
        x1 = _seg_conv(halo_sc.at[1], cb * LANES, hcw_ref, HY_W + cb * LANES, HY_CONV, seg, tm)
        hv = _seg_conv(halo_sc.at[2], cb * LANES, hcw_ref, 2 * HY_W + cb * LANES, HY_CONV, seg, tm)
        zz_ref[:, sl] = x1 * hv


def _proj(x, mod3, row0, tiles_per_row, seg, tm, p):
    t = x.shape[0]
    tok = lambda w: pl.BlockSpec((tm, w), lambda i: (i, 0))
    return pl.pallas_call(
        functools.partial(_proj_kernel, seg=seg),
        grid=(t // tm,),
        in_specs=[
            tok(D_MODEL),
            pl.BlockSpec((1, 1, 6 * D_MODEL), lambda i: (row0 + i // tiles_per_row, 0, 0)),
            _full((1, D_MODEL)),
            _resident((D_MODEL, OFF_A)),
            _resident((D_MODEL, LANES)),
            _resident((D_MODEL, 3 * HY_W)),
            _full((GDN_CONV, 3 * GDN_W)),
            _full((HY_CONV, 3 * HY_W)),
            _full((1, LANES)),
            _full((1, LANES)),
        ],
        out_specs=[tok(3 * GDN_W), tok(GDN_W), tok(LANES), tok(HY_W), tok(HY_W)],
        out_shape=[
            jax.ShapeDtypeStruct((t, 3 * GDN_W), F32),
            jax.ShapeDtypeStruct((t, GDN_W), F32),
            jax.ShapeDtypeStruct((t, LANES), F32),
            jax.ShapeDtypeStruct((t, HY_W), F32),
            jax.ShapeDtypeStruct((t, HY_W), F32),
        ],
        scratch_shapes=[pltpu.VMEM((3, GDN_W // LANES, (tm // seg) * (seg + HALO) + HALO, LANES), F32)],
        compiler_params=pltpu.CompilerParams(
            dimension_semantics=("parallel",), vmem_limit_bytes=VMEM_LIMIT),
        name="proj",
    )(x, mod3, p["n1g"], p["wq"], p["wab"], p["why"], p["cw"], p["hcw"], p["alog"], p["dtb"])


def _aligned(x, m):
    return x if isinstance(x, int) else pl.multiple_of(x, m)


def _blockdiag(y, masks):
    return jnp.concatenate([jnp.where(m, y, jnp.zeros_like(y)) for m in masks], axis=0)


def _pdot(xs, ys, masks):
    rhs = [_blockdiag(y.astype(BF16), masks) for y in ys]
    return [jnp.dot(x.astype(BF16), r, preferred_element_type=F32) for x, r in zip(xs, rhs)]


def _unit_tri_inverse(a, eye, m16, m32, m64, hm):
    add = lambda ts, ps: [t + p for t, p in zip(ts, ps)]
    sub = lambda ts, ps: [t - p for t, p in zip(ts, ps)]
    n = [-(x * m16) for x in a]
    t = [eye + x for x in n]
    n2 = _pdot(n, n, hm)
    t = add(t, _pdot(t, n2, hm))
    n4 = _pdot(n2, n2, hm)
    t = add(t, _pdot(t, n4, hm))
    n8 = _pdot(n4, n4, hm)
    t = add(t, _pdot(t, n8, hm))
    t = sub(t, _pdot(t, _pdot([x * m32 for x in a], t, hm), hm))
    t = sub(t, _pdot(t, _pdot([x * m64 for x in a], t, hm), hm))
    return t


def _gdn_kernel(q_ref, k_ref, v_ref, gb_ref, s0_ref, o_ref, sfin_ref,
                s_sc, gwh_sc, gwl_sc, bw_sc, u_sc, w_sc, qd_sc, kd_sc, qk_sc, eg_sc,
                *, nch, group):
    d = pl.program_id(1)
    j = pl.program_id(2)
    c = CHUNK
    fwd = d == 0
    nbb = q_ref.shape[0]

    @pl.when(j == 0)
    def _():
        s_sc[...] = s0_ref[:, 0]

    row = lax.broadcasted_iota(jnp.int32, (c, LANES), 0)
    lane = lax.broadcasted_iota(jnp.int32, (c, LANES), 1)
    col = lane & (c - 1)
    r = jnp.where(fwd, row, col)
    cc = jnp.where(fwd, col, row)
    tri = r >= cc
    strict = r > cc
    strict_f = strict.astype(F32)
    eye = (row == col).astype(F32)
    m16 = ((row >> 4) == (col >> 4)).astype(F32)
    same32 = ((row >> 5) == (col >> 5)).astype(F32)
    m32 = same32 - m16
    m64 = 1.0 - same32
    first = lane < c
    pm = [first, lane >= c]
    lane_w = lax.broadcasted_iota(jnp.int32, (c, GDN_W), 1)
    hw = [(lane_w >> 7) == h for h in range(HEADS)]
    pw = [((lane_w >> 7) & 1) == q for q in range(2)]
    lane_v = lax.broadcasted_iota(jnp.int32, (c, 2 * LANES), 1)
    pv = [(lane_v >> 7) == q for q in range(2)]
    r2 = lax.broadcasted_iota(jnp.int32, (c, c), 0)
    c2 = lax.broadcasted_iota(jnp.int32, (c, c), 1)
    tri_b = (jnp.where(fwd, r2, c2) >= jnp.where(fwd, c2, r2)).astype(F32).astype(BF16)
    cum_lhs = jnp.concatenate([tri_b, jnp.ones((c, c), BF16)], axis=0)
    zero_b = jnp.zeros((c, LANES), BF16)

    def pairs(x):
        return [jnp.where(first, x[:, 2 * p * LANES:(2 * p + 1) * LANES], x[:, (2 * p + 1) * LANES:(2 * p + 2) * LANES])
                for p in range(HEADS // 2)]

    def selector(first_row):
        sel_r = lax.broadcasted_iota(jnp.int32, (LANES, GDN_W), 0)
        sel_h = lax.broadcasted_iota(jnp.int32, (LANES, GDN_W), 1) >> 7
        return (sel_r == first_row + sel_h).astype(F32).astype(BF16)

    sel_g = selector(d * HEADS)
    sel_b = selector((N_DIR + d) * HEADS)
    spread = lambda x, sel: jnp.dot(x, sel, preferred_element_type=F32)
    for bb in range(nbb):
        gb_hi, gb_lo = _split2(gb_ref[bb])
        gwh_sc[bb] = spread(gb_hi, sel_g).astype(BF16)
        gwl_sc[bb] = spread(gb_lo, sel_g).astype(BF16)
        bw_sc[bb] = spread(gb_hi, sel_b) + spread(gb_lo, sel_b)

    def cumsum(lhs, hi, lo):
        return jnp.dot(lhs, hi, preferred_element_type=F32) + jnp.dot(lhs, lo, preferred_element_type=F32)

    def precompute(it, carry):
        grp = range(group)
        if nch >= group:
            seq_of = [it // (nch // group)] * group
            chunk_of = [(it - seq_of[0] * (nch // group)) * group + i for i in grp]
        else:
            seq_of = [it * (group // nch) + i // nch for i in grp]
            chunk_of = [i % nch for i in grp]
        at = [(seq_of[i], pl.ds(_aligned(chunk_of[i] * c, c), c)) for i in grp]
        items = [(i, p) for i in grp for p in range(HEADS // 2)]
        kc = [k_ref[bb, r_, :] for bb, r_ in at]
        qc = [q_ref[bb, r_, :] for bb, r_ in at]
        gh = [gwh_sc[bb, r_, :] for bb, r_ in at]
        gl = [gwl_sc[bb, r_, :] for bb, r_ in at]
        bw = [bw_sc[bb, r_, :] for bb, r_ in at]
        gph = [pairs(x) for x in gh]
        gpl = [pairs(x) for x in gl]
        bp = [pairs(x) for x in bw]
        e = [cumsum(tri_b, jnp.where(strict, gph[i][p], zero_b), jnp.where(strict, gpl[i][p], zero_b))
             for i, p in items]
        kb = [x.astype(BF16) for x in kc]
        kq = [lax.dot_general(jnp.concatenate([kb[i], qc[i].astype(BF16)], axis=0), _blockdiag(kb[i], hw),
                              (((1,), (1,)), ((), ())), preferred_element_type=F32) for i in grp]
        decay = [jnp.where(tri, jnp.exp(x), 0.0) for x in e]
        a = [bp[i][p] * kq[i][:c, p * LANES:(p + 1) * LANES] * decay[k] * strict_f
             for k, (i, p) in enumerate(items)]
        t = _unit_tri_inverse(a, eye, m16, m32, m64, pm)
        cums =[cumsum(cum_lhs, gh[i], gl[i]) for i in grp]
        gc = [x[:c] for x in cums]
        gt = [x[c:] for x in cums]
        egc = [jnp.exp(x) for x in gc]
        vb = [v_ref[at[i][0], at[i][1], :] * bw[i] for i in grp]
        kbg = [kc[i] * bw[i] * egc[i] for i in grp]
        uw = _pdot(t, [jnp.concatenate([vb[i][:, 2 * p * LANES:(2 * p + 2) * LANES],
                                        kbg[i][:, 2 * p * LANES:(2 * p + 2) * LANES]], axis=1)
                       for i, p in items], pw)
        for k, (i, p) in enumerate(items):
            bb, r_ = at[i]
            u_sc[bb, r_, 2 * p * LANES:(2 * p + 2) * LANES] = uw[k][:, :2 * LANES]
            w_sc[bb, r_, 2 * p * LANES:(2 * p + 2) * LANES] = uw[k][:, 2 * LANES:].astype(BF16)
            qk_sc[bb, r_, p * LANES:(p + 1) * LANES] = (kq[i][c:, p * LANES:(p + 1) * LANES] * decay[k]).astype(BF16)
        for i in grp:
            bb, r_ = at[i]
            qd_sc[bb, r_, :] = (qc[i] * egc[i]).astype(BF16)
            kd_sc[bb, r_, :] = (kc[i] * jnp.exp(gt[i] - gc[i])).astype(BF16)
            eg_sc[bb, pl.ds(_aligned(chunk_of[i] * 8, 8), 8), :] = jnp.exp(gt[i][0:8, :])
        return carry

    lax.fori_loop(0, nbb * nch // group, precompute, 0)

    def recur(n, carry):
        ne = jnp.where(fwd, n, nch - 1 - n)
        rows = pl.ds(pl.multiple_of(ne * c, c), c)
        seqs = range(nbb)
        heads = [slice(h * DK, (h + 1) * DK) for h in range(HEADS)]
        s_old = [[s_sc[bb, h] for h in range(HEADS)] for bb in seqs]
        both = [[jnp.dot(jnp.concatenate([w_sc[bb, rows, hs], qd_sc[bb, rows, hs]], axis=0),
                         s_old[bb][h].astype(BF16), preferred_element_type=F32)
                 for h, hs in enumerate(heads)] for bb in seqs]
        vb = [jnp.concatenate([u_sc[bb, rows, hs] - both[bb][h][:c] for h, hs in enumerate(heads)],
                              axis=1).astype(BF16) for bb in seqs]
        upd = [[lax.dot_general(kd_sc[bb, rows, hs], vb[bb][:, hs], (((0,), (0,)), ((), ())),
                                preferred_element_type=F32) for hs in heads] for bb in seqs]
        for bb in seqs:
            eg = eg_sc[bb, pl.ds(pl.multiple_of(ne * 8, 8), 8), :][0:1, :]
            for h, hs in enumerate(heads):
                s_sc[bb, h] = s_old[bb][h] * eg[:, hs] + upd[bb][h]
        for bb in seqs:
            qkv = [jnp.dot(qk_sc[bb, rows, p * LANES:(p + 1) * LANES],
                           _blockdiag(vb[bb][:, 2 * p * LANES:(2 * p + 2) * LANES], pv),
                           preferred_element_type=F32) for p in range(HEADS // 2)]
            o_ref[bb, 0, rows, :] = (jnp.concatenate([both[bb][h][c:] for h in range(HEADS)], axis=1)
                                     + jnp.concatenate(qkv, axis=1))
        return carry

    lax.fori_loop(0, nch, recur, 0)

    @pl.when(j == pl.num_programs(2) - 1)
    def _():
        sfin_ref[:, 0] = s_sc[...]


def _gdn(qkv, gb, s0, nb, seq):
    lb = min(seq, GDN_BLOCK)
    nblk = seq // lb
    nch = lb // CHUNK

    nbb = GDN_BATCH

    def blk(d, j):
        return j + d * (nblk - 1 - 2 * j)

    def qkv_spec(part):
        return pl.BlockSpec((nbb, lb, GDN_W), lambda b, d, j: (b, blk(d, j), part))

    state_spec = pl.BlockSpec((nbb, 1, HEADS, DK, DK), lambda b, d, j: (b, d, 0, 0, 0))
    qkv3 = qkv.reshape(nb, seq, 3 * GDN_W)
    return pl.pallas_call(
        functools.partial(_gdn_kernel, nch=nch, group=min(nbb * nch, GDN_GROUP)),
        grid=(nb // nbb, N_DIR, nblk),
        in_specs=[
            qkv_spec(0), qkv_spec(1), qkv_spec(2),
            pl.BlockSpec((nbb, lb, LANES), lambda b, d, j: (b, blk(d, j), 0)),
            state_spec,
        ],
        out_specs=[
            pl.BlockSpec((nbb, 1, lb, GDN_W), lambda b, d, j: (b, d, blk(d, j), 0)),
            state_spec,
        ],
        out_shape=[
            jax.ShapeDtypeStruct((nb, N_DIR, seq, GDN_W), F32),
            jax.ShapeDtypeStruct((nb, N_DIR, HEADS, DK, DK), F32),
        ],
        scratch_shapes=[
            pltpu.VMEM((nbb, HEADS, DK, DK), F32),
            pltpu.VMEM((nbb, lb, GDN_W), BF16),
            pltpu.VMEM((nbb, lb, GDN_W), BF16),
            pltpu.VMEM((nbb, lb, GDN_W), F32),
            pltpu.VMEM((nbb, lb, GDN_W), F32),
            pltpu.VMEM((nbb, lb, GDN_W), BF16),
            pltpu.VMEM((nbb, lb, GDN_W), BF16),
            pltpu.VMEM((nbb, lb, GDN_W), BF16),
            pltpu.VMEM((nbb, lb, PACK), BF16),
            pltpu.VMEM((nbb, nch * 8, GDN_W), F32),
        ],
        compiler_params=pltpu.CompilerParams(
            dimension_semantics=("parallel", "arbitrary", "arbitrary"), vmem_limit_bytes=VMEM_LIMIT),
        name="gdn",
    )(qkv3, qkv3, qkv3, gb.reshape(nb, seq, LANES), s0)


def _hy_dims(seq):
    nb = min(HY_BLOCK, seq)
    n1 = 2 * seq // nb
    return nb, n1, n1 // 2, n1 // 2 + 1


def _hy_consts(seq):
    nb, n1, nt1, nk = _hy_dims(seq)
    n = 2 * seq
    k2 = np.arange(nb)[:, None]
    t2 = np.arange(nb)[None, :]
    ang = 2.0 * np.pi * ((k2 * t2) % nb) / nb
    fr, fi = np.cos(ang), -np.sin(ang)
    fwd = np.block([[fr, -fi], [fi, fr]])
    inv = np.block([[fr, fi], [-fi, fr]])
    k1 = np.arange(nk)[:, None]
    ang = 2.0 * np.pi * (k1 * np.arange(nb)[None, :]) / n
    twr = np.repeat(np.cos(ang)[:, :, None], LANES, axis=2)
    twi = np.repeat(-np.sin(ang)[:, :, None], LANES, axis=2)
    return dict(
        fwd=jnp.asarray(fwd, F32).astype(BF16), inv=jnp.asarray(inv, F32).astype(BF16),
        twr=jnp.asarray(twr, F32), twi=jnp.asarray(twi, F32))


def _lane_tile(x, width):
    return x if width == LANES else jnp.concatenate([x] * (width // LANES), axis=1)


def _unit_roots(n1):
    ang = 2.0 * np.pi * np.arange(n1) / n1
    return [float(v) for v in np.round(np.cos(ang), 15)], [float(v) for v in np.round(np.sin(ang), 15)]


def _lincomb(terms):
    groups = {}
    for cf, x in terms:
        if cf != 0.0:
            groups.setdefault(round(abs(cf), 12), ([], []))[0 if cf > 0 else 1].append(x)
    total = None
    for mag, (pos, neg) in groups.items():
        acc = None
        for x in pos:
            acc = x if acc is None else acc + x
        for x in neg:
            acc = -x if acc is None else acc - x
        if mag != 1.0:
            acc = mag * acc
        total = acc if total is None else total + acc
    return total


def _hy_forward(blocks, k1, twr_ref, twi_ref, fwd_ref, nt1, nb, n1):
    cosn, sinn = _unit_roots(n1)
    cs = []
    for block in blocks:
        xs = [block(t1) for t1 in range(nt1)]
        cs.append((_lincomb([(cosn[(t1 * k1) % n1], xs[t1]) for t1 in range(nt1)]),
                   _lincomb([(-sinn[(t1 * k1) % n1], xs[t1]) for t1 in range(nt1)])))
    if k1 == 0:
        ys = [jnp.dot(fwd_ref[:, :nb], cr.astype(BF16), preferred_element_type=F32) for cr, _ in cs]
    else:
        width = cs[0][0].shape[1]
        twr = _lane_tile(twr_ref[k1], width)
        twi = _lane_tile(twi_ref[k1], width)
        c2 = [jnp.concatenate([cr * twr, cr * twi] if ci is None else
                              [cr * twr - ci * twi, cr * twi + ci * twr], axis=0).astype(BF16) for cr, ci in cs]
        ys = [jnp.dot(fwd_ref[...], x, preferred_element_type=F32) for x in c2]
    return [(y[:nb], y[nb:]) for y in ys]


def _hyena_kernel(zz_ref, x0_ref, skip_ref, hr_ref, hi_ref, twr_ref, twi_ref,
                  fwd_ref, inv_ref, o_ref, zc_sc, *, nb, n1, nt1, nk, seq, sps):
    zc_sc[...] = jnp.zeros_like(zc_sc)
    cosn, sinn = _unit_roots(n1)
    for k1 in range(nk):
        ys = _hy_forward([lambda t1, s=s: zz_ref[s * seq + t1 * nb:s * seq + (t1 + 1) * nb, :]
                          for s in range(sps)], k1, twr_ref, twi_ref, fwd_ref, nt1, nb, n1)
        hr = hr_ref[k1]
        hi = hi_ref[k1]
        z2 = [jnp.concatenate([yr * hr - yi * hi, yr * hi + yi * hr], axis=0).astype(BF16) for yr, yi in ys]
        dd = [jnp.dot(inv_ref[...], x, preferred_element_type=F32) for x in z2]
        wt = (1.0 if k1 in (0, n1 // 2) else 2.0) / (n1 * nb)
        for s in range(sps):
            dr, di = dd[s][:nb], dd[s][nb:]
            if k1 == 0:
                er, ei = dr, di
            else:
                twr = _lane_tile(twr_ref[k1], hr.shape[1])
                twi = _lane_tile(twi_ref[k1], hr.shape[1])
                er = dr * twr + di * twi
                ei = di * twr - dr * twi
            scaled = {}

            def times(mag, name, x):
                key = (name, round(mag, 12))
                if key not in scaled:
                    scaled[key] = x if mag == 1.0 else mag * x
                return scaled[key]

            for t1 in range(nt1):
                add = None
                for cf, name, x in ((wt * cosn[(t1 * k1) % n1], "r", er), (-wt * sinn[(t1 * k1) % n1], "i", ei)):
                    if cf != 0.0:
                        term = times(abs(cf), name, x)
                        if add is None:
                            add = term if cf > 0 else -term
                        else:
                            add = add + term if cf > 0 else add - term
                rows = slice(s * seq + t1 * nb, s * seq + (t1 + 1) * nb)
                zc_sc[rows, :] += add
    zz = zz_ref[...]
    o_ref[...] = x0_ref[...] * (zc_sc[...] + zz * skip_ref[...])


def _hyena(zz, x0, skip, hr, hi, consts, nb_batch, seq):
    nb, n1, nt1, nk = _hy_dims(seq)
    ct = HY_CT
    nct = HY_W // ct
    sps = max(1, min(nb_batch, HY_ROWS // seq, HY_SEQS))
    rows = sps * seq
    tok = pl.BlockSpec((rows, ct), lambda c, b: (b, c))
    one = dict(pipeline_mode=pl.Buffered(1)) if rows >= HY_ROWS else {}
    tok_in = pl.BlockSpec((rows, ct), lambda c, b: (b, c), **one)
    spec = pl.BlockSpec((nk, nb, ct), lambda c, b: (0, 0, c), pipeline_mode=pl.Buffered(1))
    return pl.pallas_call(
        functools.partial(_hyena_kernel, nb=nb, n1=n1, nt1=nt1, nk=nk, seq=seq, sps=sps),
        grid=(nct, nb_batch // sps),
        in_specs=[
            tok_in, tok_in,
            pl.BlockSpec((1, ct), lambda c, b: (0, c)),
            spec, spec,
            _resident((nk, nb, LANES)), _resident((nk, nb, LANES)),
            _resident((2 * nb, 2 * nb)), _resident((2 * nb, 2 * nb)),
        ],
        out_specs=tok,
        out_shape=jax.ShapeDtypeStruct((nb_batch * seq, HY_W), F32),
        scratch_shapes=[pltpu.VMEM((rows, ct), F32)],
        compiler_params=pltpu.CompilerParams(
            dimension_semantics=("parallel", "parallel"), vmem_limit_bytes=VMEM_LIMIT),
        name="hyena",
    )(zz, x0, skip, hr, hi, consts["twr"], consts["twi"], consts["fwd"], consts["inv"])


def _filter_kernel(z_ref, tcol_ref, w1_ref, b1_ref, fq_ref, w2_ref, b2_ref,
                   w3a_ref, w3b_ref, b3a_ref, b3b_ref, dl_ref, twr_ref, twi_ref, fwd_ref,
                   hr_ref, hi_ref, hh_sc, a2_sc, *, nb, n1, nk):
    ct = hr_ref.shape[-1]

    @pl.when(pl.program_id(1) == 0)
    def _():
        fq = fq_ref[0]
        a1 = jnp.sin(fq * (_dot_hp(z_ref[...], w1_ref[0]) + b1_ref[0]))
        a2_sc[...] = jnp.sin(fq * (_dot_hp(a1, w2_ref[0]) + b2_ref[0]))

    a2 = a2_sc[...]
    win = jnp.exp(-tcol_ref[...] * dl_ref[...])
    hh_sc[:, :ct] = (_dot_hp(a2, w3a_ref[0]) + b3a_ref[0]) * win
    h1 = (_dot_hp(a2, w3b_ref[0]) + b3b_ref[0]) * win
    rowid = lax.broadcasted_iota(jnp.int32, h1.shape, 0)
    hh_sc[:, ct:] = jnp.where(rowid == 0, 0.0, h1)

    for k1 in range(nk):
        ((yr, yi),) = _hy_forward([lambda t1: hh_sc[t1 * nb:(t1 + 1) * nb, :]],
                                  k1, twr_ref, twi_ref, fwd_ref, n1 // 2, nb, n1)
        hr_ref[0, k1] = yr[:, :ct] + yr[:, ct:]
        hi_ref[0, k1] = yi[:, :ct] - yi[:, ct:]


def _filter_feats(seq):
    t = np.linspace(0.0, 1.0, seq)[:, None]
    bands = (HY_EMB - 1) // 2
    f = np.linspace(1e-4, bands - 1, bands)[None, :]
    wpos = 2.0 * np.pi * np.arange(seq)[:, None] / seq
    z = np.concatenate([t, np.cos(f * wpos), -np.sin(f * wpos)], axis=-1)
    z = np.pad(z, ((0, 0), (0, LANES - HY_EMB)))
    tcol = np.repeat(t, LANES, axis=1)
    return jnp.asarray(z, F32), jnp.asarray(tcol, F32)


def _filter_spectra(seq, consts, fp):
    nb, n1, nt1, nk = _hy_dims(seq)
    ct = FILTER_CT
    nct = HY_W // ct
    z, tcol = _filter_feats(seq)
    per_layer = lambda r, c: pl.BlockSpec((1, r, c), lambda l, j: (l, 0, 0))
    out = pl.BlockSpec((1, nk, nb, ct), lambda l, j: (l, 0, 0, j))
    return pl.pallas_call(
        functools.partial(_filter_kernel, nb=nb, n1=n1, nk=nk),
        grid=(DEPTH, nct),
        in_specs=[
            _full((seq, LANES)), _full((seq, LANES)),
            per_layer(LANES, LANES), per_layer(1, LANES), per_layer(1, LANES),
            per_layer(LANES, LANES), per_layer(1, LANES),
            pl.BlockSpec((1, LANES, ct), lambda l, j: (l, 0, j)),
            pl.BlockSpec((1, LANES, ct), lambda l, j: (l, 0, nct + j)),
            pl.BlockSpec((1, 1, ct), lambda l, j: (l, 0, j)),
            pl.BlockSpec((1, 1, ct), lambda l, j: (l, 0, nct + j)),
            pl.BlockSpec((1, ct), lambda l, j: (0, j)),
            _full((nk, nb, LANES)), _full((nk, nb, LANES)), _full((2 * nb, 2 * nb)),
        ],
        out_specs=[out, out],
        out_shape=[jax.ShapeDtypeStruct((DEPTH, nk, nb, HY_W), F32)] * 2,
        scratch_shapes=[pltpu.VMEM((seq, 2 * ct), F32), pltpu.VMEM((seq, LANES), F32)],
        compiler_params=pltpu.CompilerParams(
            dimension_semantics=("arbitrary", "arbitrary"), vmem_limit_bytes=VMEM_LIMIT),
        name="hyena_filter",
    )(z, tcol, fp["w1"], fp["b1"], fp["fq"], fp["w2"], fp["b2"],
      fp["w3"], fp["w3"], fp["b3"], fp["b3"], fp["deltas"], consts["twr"], consts["twi"], consts["fwd"])


def _rms(x, g):
    return x * lax.rsqrt(jnp.mean(x * x, axis=-1, keepdims=True) + EPS) * g


def _mlp_kernel(x_ref, of_ref, ob_ref, gate_ref, yh_ref, mod_ref, gng_ref, n2g_ref, fg_ref,
                wout_ref, w1_ref, w2_ref, o_ref, *, final):
    mod = mod_ref[0]
    gate1 = mod[:, 2 * D_MODEL:3 * D_MODEL]
    shift2 = mod[:, 3 * D_MODEL:4 * D_MODEL]
    scale2 = mod[:, 4 * D_MODEL:5 * D_MODEL]
    gate2 = mod[:, 5 * D_MODEL:6 * D_MODEL]
    o = of_ref[0, 0] + ob_ref[0, 0]
    gate = gate_ref[...]
    mix = jnp.dot(yh_ref[...].astype(BF16), wout_ref[GDN_W:, :], preferred_element_type=F32)
    for h in range(HEADS):
        hs = slice(h * DK, (h + 1) * DK)
        oh = _rms(o[:, hs], gng_ref[...]) * _silu(gate[:, hs])
        mix = mix + jnp.dot(oh.astype(BF16), wout_ref[hs, :], preferred_element_type=F32)
    x = x_ref[...] + gate1 * mix
    h2 = (_rms(x, n2g_ref[...]) * (1.0 + scale2) + shift2).astype(BF16)
    acc = None
    for cb in range(D_FF // D_MODEL):
        cs = slice(cb * D_MODEL, (cb + 1) * D_MODEL)
        a = jnp.maximum(jnp.dot(h2, w1_ref[:, cs], preferred_element_type=F32), 0.0)
        part = jnp.dot((a * a).astype(BF16), w2_ref[cs, :], preferred_element_type=F32)
        acc = part if acc is None else acc + part
    x = x + gate2 * acc
    if final:
        x = _rms(x, fg_ref[...])
    o_ref[...] = x


def _mlp(x, o, gate, yh, mod3, row0, tiles_per_row, tiles_per_seq, tm, p, final):
    t = x.shape[0]
    tok = lambda w: pl.BlockSpec((tm, w), lambda i: (i, 0))

    def o_spec(d):
        return pl.BlockSpec((1, 1, tm, GDN_W), lambda i: (i // tiles_per_seq, d, i % tiles_per_seq, 0))

    return pl.pallas_call(
        functools.partial(_mlp_kernel, final=final),
        grid=(t // tm,),
        in_specs=[
            tok(D_MODEL), o_spec(0), o_spec(1), tok(GDN_W), tok(HY_W),
            pl.BlockSpec((1, 1, 6 * D_MODEL), lambda i: (row0 + i // tiles_per_row, 0, 0)),
            _full((1, DK)), _full((1, D_MODEL)), _full((1, D_MODEL)),
            _resident((D_MODEL, D_MODEL)), _resident((D_MODEL, D_FF)), _resident((D_FF, D_MODEL)),
        ],
        out_specs=tok(D_MODEL),
        out_shape=jax.ShapeDtypeStruct((t, D_MODEL), F32),
        compiler_params=pltpu.CompilerParams(
            dimension_semantics=("parallel",), vmem_limit_bytes=VMEM_LIMIT),
        name="mlp",
    )(x, o, o, gate, yh, mod3, p["gng"], p["n2g"], p["fg"], p["wout"], p["w1"], p["w2"])


def _pad_to(a, shape):
    return jnp.pad(a, [(0, s - d) for d, s in zip(a.shape, shape)])


def kernel(x_prompt, x_sample, state_gdn, c, c_ctx, w_ada, b_ada, norm1_g, norm2_g, w_in, gdn_conv_w, gdn_a_log, gdn_dt_bias, gdn_norm_g, hy_conv_w, hy_w1, hy_b1, hy_freq, hy_w2, hy_b2, hy_w3, hy_b3, hy_skip, w_out, w_mlp1, w_mlp2, final_g):
    nb_ctx, seq_ctx, _ = x_prompt.shape
    nb_smp, seq_smp, _ = x_sample.shape
    ctx_row = nb_smp

    cvec = _pad_to(jnp.concatenate([c, c_ctx[None]], axis=0), (MOD_ROWS, D_MODEL))
    mod = _modulation(cvec, w_ada, b_ada)

    layers = []
    for l in range(DEPTH):
        layers.append(dict(
            n1g=norm1_g[l][None], n2g=norm2_g[l][None], gng=gdn_norm_g[l][None], fg=final_g[None],
            wq=w_in[l][:, :OFF_A].astype(BF16),
            wab=_pad_to(w_in[l][:, OFF_A:OFF_HY], (D_MODEL, LANES)).astype(BF16),
            why=w_in[l][:, OFF_HY:].astype(BF16),
            cw=gdn_conv_w[l], hcw=hy_conv_w[l],
            alog=_pad_to(gdn_a_log[l].reshape(1, -1), (1, LANES)),
            dtb=_pad_to(gdn_dt_bias[l].reshape(1, -1), (1, LANES)),
            skip=hy_skip[l][None],
            wout=w_out[l].astype(BF16), w1=w_mlp1[l].astype(BF16), w2=w_mlp2[l].astype(BF16),
        ))

    max_decay = math.log(1e-2) / 0.3
    min_decay = math.log(1e-2) / 1.5
    fparams = dict(
        w1=_pad_to(hy_w1, (DEPTH, LANES, LANES)), b1=_pad_to(hy_b1[:, None], (DEPTH, 1, LANES)),
        fq=_pad_to(hy_freq[:, None], (DEPTH, 1, LANES)),
        w2=_pad_to(hy_w2, (DEPTH, LANES, LANES)), b2=_pad_to(hy_b2[:, None], (DEPTH, 1, LANES)),
        w3=_pad_to(hy_w3, (DEPTH, LANES, N_DIR * HY_W)), b3=hy_b3[:, None],
        deltas=jnp.asarray(np.abs(np.linspace(min_decay, max_decay, HY_W))[None], F32),
    )

    def run_group(x3, s0_of_layer, row0, per_batch_mod, seg):
        nb, seq, _ = x3.shape
        consts = _hy_consts(seq)
        spec_r, spec_i = _filter_spectra(seq, consts, fparams)
        x = x3.reshape(nb * seq, D_MODEL)
        proj_tile = min(seq, TOKEN_TILE)
        tiles_per_seq = seq // proj_tile
        tiles_per_row = tiles_per_seq if per_batch_mod else nb * tiles_per_seq
        mlp_tile = min(seq, MLP_TILE)
        mlp_tps = seq // mlp_tile
        states = []
        for l in range(DEPTH):
            p = layers[l]
            mod3 = mod[l].reshape(MOD_ROWS, 1, 6 * D_MODEL)
            qkv, gate, gb, x0, zz = _proj(x, mod3, row0, tiles_per_row, seg, proj_tile, p)
            o, s_fin = _gdn(qkv, gb, s0_of_layer(l), nb, seq)
            yh = _hyena(zz, x0, p["skip"], spec_r[l], spec_i[l], consts, nb, seq)
            x = _mlp(x, o, gate, yh, mod3, row0, tiles_per_row * mlp_tps // tiles_per_seq, mlp_tps, mlp_tile,
                     p, l == DEPTH - 1)
            states.append(s_fin)
        return x.reshape(nb, seq, D_MODEL), states

    zero_state = jnp.zeros((nb_ctx, N_DIR, HEADS, DK, DK), F32)
    y_prompt, ctx_states = run_group(x_prompt, lambda l: zero_state, ctx_row, False, seq_ctx)
    new_state = jnp.stack(ctx_states, axis=1).astype(x_prompt.dtype)
    y_sample, _ = run_group(x_sample, lambda l: state_gdn[:, l].astype(F32), 0, True, GRID_W)
    return (y_prompt, y_sample, new_state)
```
